```python
import math
import jax, jax.numpy as jnp
from jax import lax
import numpy as np

D_MODEL = 1024
BATCH = 4
SEQ = 4096
DEPTH = 2
DEC_BATCH = 16
DEC_SEQ = 64
PAST_LEN = 4096

CHUNK = 64
C_CONV = D_MODEL // 2
CONV_WIDTH = 31
HEAD_DIM = 64
N_HEADS = (D_MODEL // 2) // HEAD_DIM
N_KV_HEADS = 2
GROUP = N_HEADS // N_KV_HEADS
ATTN_WIDTH = N_HEADS * HEAD_DIM
MIX_WIDTH = C_CONV + ATTN_WIDTH
N_IDX_HEADS = 8
IDX_DIM = 64
TOPK_MAX = 256
ROPE_THETA = 10000.0
QUERY_BLOCK = 128
D_FF = -(-8 * D_MODEL // (3 * 256)) * 256
IN_DIM = 2 * C_CONV + (N_HEADS + 2 * N_KV_HEADS) * HEAD_DIM + N_IDX_HEADS * IDX_DIM + IDX_DIM + N_IDX_HEADS
ALPHA = (2 * DEPTH) ** 0.25
BETA = (8 * DEPTH) ** -0.25
ATTN_SCALE = HEAD_DIM ** -0.5
IDX_SCALE = IDX_DIM ** -0.5
IDX_HEAD_SCALE = N_IDX_HEADS ** -0.5
LN_EPS = 1e-5

kernel_name = "hybrid_conv_dsa_stream_step"


def layer_norm(x, g, b):
    xf = x.astype(jnp.float32)
    mu = jnp.mean(xf, axis=-1, keepdims=True)
    var = jnp.mean(jnp.square(xf - mu), axis=-1, keepdims=True)
    y = (xf - mu) * lax.rsqrt(var + LN_EPS) * g.astype(jnp.float32) + b.astype(jnp.float32)
    return y.astype(x.dtype)


def rope(x, pos):
    half = x.shape[-1] // 2
    inv = ROPE_THETA ** (-jnp.arange(half, dtype=jnp.float32) / half)
    ang = pos.astype(jnp.float32)[:, None] * inv[None, :]
    cos = jnp.cos(ang)[None, :, None, :]
    sin = jnp.sin(ang)[None, :, None, :]
    xf = x.astype(jnp.float32)
    x1, x2 = xf[..., :half], xf[..., half:]
    return jnp.concatenate([x1 * cos - x2 * sin, x2 * cos + x1 * sin], axis=-1).astype(x.dtype)


def conv_mixer(u, conv_state, conv_w, conv_b, ln_g, ln_b):
    full = jnp.concatenate([conv_state.astype(u.dtype), u], axis=1)
    y = lax.conv_general_dilated(full, conv_w[:, None, :].astype(u.dtype), window_strides=(1,),
                                 padding='VALID', dimension_numbers=('NWC', 'WIO', 'NWC'),
                                 feature_group_count=C_CONV) + conv_b
    y = jax.nn.silu(layer_norm(y, ln_g, ln_b))
    return y, full[:, -(CONV_WIDTH - 1):]


def dsa_attention(q, k, v, qi, ki, wi, q_pos, topk):
    B, T = q.shape[0], q.shape[1]
    S = k.shape[1]
    qb = min(QUERY_BLOCK, T)
    nb = T // qb
    key_chunk = jnp.arange(S, dtype=jnp.int32) // CHUNK
    ki32 = ki.astype(jnp.float32)
    bidx = jnp.arange(B)[:, None, None]

    def block(args):
        q_b, qi_b, wi_b, pos_b = args
        q_chunk = pos_b // CHUNK
        admit = key_chunk[None, :] <= q_chunk[:, None]
        dots = jnp.einsum('bqhd,bsd->bqhs', qi_b.astype(jnp.float32), ki32) * IDX_SCALE
        score = jnp.einsum('bqh,bqhs->bqs', wi_b.astype(jnp.float32), jax.nn.relu(dots))
        score = jnp.where(admit[None], score, -jnp.inf)
        _, idx = lax.top_k(score, topk)
        valid = (idx // CHUNK) <= q_chunk[None, :, None]
        kg = k[bidx, idx].astype(jnp.float32)
        vg = v[bidx, idx].astype(jnp.float32)
        qg = q_b.reshape(B, qb, N_KV_HEADS, GROUP, HEAD_DIM).astype(jnp.float32)
        logits = jnp.einsum('bqgrd,bqkgd->bqgrk', qg, kg) * ATTN_SCALE
        logits = jnp.where(valid[:, :, None, None, :], logits, -jnp.inf)
        p = jax.nn.softmax(logits, axis=-1)
        o = jnp.einsum('bqgrk,bqkgd->bqgrd', p, vg)
        return o.reshape(B, qb, ATTN_WIDTH).astype(q.dtype)

    def split(a):
        return jnp.moveaxis(a.reshape((B, nb, qb) + a.shape[2:]), 1, 0)

    out = lax.map(block, (split(q), split(qi), split(wi), q_pos.reshape(nb, qb)))
    return jnp.moveaxis(out, 0, 1).reshape(B, T, ATTN_WIDTH)


def trunk_layer(x, conv_state, k_past, v_past, ki_past,
                w_in, conv_w, conv_b, cln_g, cln_b, w_o, ln1_g, ln1_b,
                w_gate_up, w_down, ln2_g, ln2_b):
    B, T, _ = x.shape
    past = k_past.shape[1]
    pos = jnp.arange(T, dtype=jnp.int32) + past
    sizes = (C_CONV, C_CONV, ATTN_WIDTH, N_KV_HEADS * HEAD_DIM, N_KV_HEADS * HEAD_DIM,
             N_IDX_HEADS * IDX_DIM, IDX_DIM, N_IDX_HEADS)
    points = [sum(sizes[:i + 1]) for i in range(len(sizes) - 1)]
    h = x @ w_in
    a, g, q, k, v, qi, ki, wi = jnp.split(h, points, axis=-1)
    u = a * jax.nn.sigmoid(g)
    conv_out, new_conv = conv_mixer(u, conv_state, conv_w, conv_b, cln_g, cln_b)
    q = rope(q.reshape(B, T, N_HEADS, HEAD_DIM), pos)
    k = rope(k.reshape(B, T, N_KV_HEADS, HEAD_DIM), pos)
    v = v.reshape(B, T, N_KV_HEADS, HEAD_DIM)
    qi = rope(qi.reshape(B, T, N_IDX_HEADS, IDX_DIM), pos)
    ki = rope(ki[:, :, None, :], pos)[:, :, 0]
    wi = wi * IDX_HEAD_SCALE
    k_all = jnp.concatenate([k_past.astype(k.dtype), k], axis=1)
    v_all = jnp.concatenate([v_past.astype(v.dtype), v], axis=1)
    ki_all = jnp.concatenate([ki_past.astype(ki.dtype), ki], axis=1)
    topk = min(TOPK_MAX, k_all.shape[1] // 4)
    attn_out = dsa_attention(q, k_all, v_all, qi, ki_all, wi, pos, topk)
    mix = jnp.concatenate([conv_out, attn_out], axis=-1) @ w_o
    x = layer_norm(ALPHA * x + mix, ln1_g, ln1_b)
    gate, up = jnp.split(x @ w_gate_up, 2, axis=-1)
    ffn = (jax.nn.silu(gate) * up) @ w_down
    x = layer_norm(ALPHA * x + ffn, ln2_g, ln2_b)
    return x, k, v, ki, new_conv


def setup_inputs(seed: int = 0) -> dict:
    key = jax.random.key(seed)
    ks = jax.random.split(key, 20)

    def nrm(k, shape, s):
        return jax.random.normal(k, shape, jnp.float32) * s

    return {
        "x_prompt": nrm(ks[0], (BATCH, SEQ, D_MODEL), 1.0),
        "x_sample": nrm(ks[1], (DEC_BATCH, DEC_SEQ, D_MODEL), 1.0),
        "cache_k": nrm(ks[2], (DEPTH, DEC_BATCH, PAST_LEN, N_KV_HEADS, HEAD_DIM), 1.0),
        "cache_v": nrm(ks[3], (DEPTH, DEC_BATCH, PAST_LEN, N_KV_HEADS, HEAD_DIM), 1.0),
        "cache_k_idx": nrm(ks[4], (DEPTH, DEC_BATCH, PAST_LEN, IDX_DIM), 1.0),
        "state_conv": nrm(ks[5], (DEPTH, DEC_BATCH, CONV_WIDTH - 1, C_CONV), 0.5),
        "w_in": nrm(ks[6], (DEPTH, D_MODEL, IN_DIM), D_MODEL ** -0.5),
        "conv_w": nrm(ks[7], (DEPTH, CONV_WIDTH, C_CONV), CONV_WIDTH ** -0.5),
        "conv_b": nrm(ks[8], (DEPTH, C_CONV), 0.02),
        "conv_ln_g": 1.0 + nrm(ks[9], (DEPTH, C_CONV), 0.02),
        "conv_ln_b": nrm(ks[10], (DEPTH, C_CONV), 0.02),
        "w_o": nrm(ks[11], (DEPTH, MIX_WIDTH, D_MODEL), MIX_WIDTH ** -0.5 * BETA),
        "ln1_g": 1.0 + nrm(ks[12], (DEPTH, D_MODEL), 0.02),
        "ln1_b": nrm(ks[13], (DEPTH, D_MODEL), 0.02),
        "w_gate_up": nrm(ks[14], (DEPTH, D_MODEL, 2 * D_FF), D_MODEL ** -0.5),
        "w_down": nrm(ks[15], (DEPTH, D_FF, D_MODEL), D_FF ** -0.5 * BETA),
        "ln2_g": 1.0 + nrm(ks[16], (DEPTH, D_MODEL), 0.02),
        "ln2_b": nrm(ks[17], (DEPTH, D_MODEL), 0.02),
    }


def reference(x_prompt, x_sample, cache_k, cache_v, cache_k_idx, state_conv,
              w_in, conv_w, conv_b, conv_ln_g, conv_ln_b, w_o, ln1_g, ln1_b,
              w_gate_up, w_down, ln2_g, ln2_b):
    Bp = x_prompt.shape[0]
    dt = x_prompt.dtype
    empty_kv = jnp.zeros((Bp, 0, N_KV_HEADS, HEAD_DIM), dt)
    empty_ki = jnp.zeros((Bp, 0, IDX_DIM), dt)
    zero_conv = jnp.zeros((Bp, CONV_WIDTH - 1, C_CONV), dt)
    hp, hs = x_prompt, x_sample
    kp, vp, kip, cp = [], [], [], []
    ksl, vsl, kisl, csl = [], [], [], []
    for l in range(DEPTH):
        w = (w_in[l], conv_w[l], conv_b[l], conv_ln_g[l], conv_ln_b[l], w_o[l],
             ln1_g[l], ln1_b[l], w_gate_up[l], w_down[l], ln2_g[l], ln2_b[l])
        hp, k1, v1, ki1, c1 = trunk_layer(hp, zero_conv, empty_kv, empty_kv, empty_ki, *w)
        hs, k2, v2, ki2, c2 = trunk_layer(hs, state_conv[l], cache_k[l], cache_v[l], cache_k_idx[l], *w)
        kp.append(k1); vp.append(v1); kip.append(ki1); cp.append(c1)
        ksl.append(k2); vsl.append(v2); kisl.append(ki2); csl.append(c2)
    return (hp, hs,
            jnp.stack(kp), jnp.stack(vp), jnp.stack(kip), jnp.stack(cp),
            jnp.stack(ksl), jnp.stack(vsl), jnp.stack(kisl), jnp.stack(csl))
```

```python
import functools
import math

import jax
import jax.numpy as jnp
from jax import lax
from jax.experimental import pallas as pl
from jax.experimental.pallas import tpu as pltpu

D_MODEL = 1024
CHUNK = 64
C_CONV = D_MODEL // 2
CONV_WIDTH = 31
HEAD_DIM = 64
N_HEADS = (D_MODEL // 2) // HEAD_DIM
N_KV_HEADS = 2
GROUP = N_HEADS // N_KV_HEADS
ATTN_WIDTH = N_HEADS * HEAD_DIM
N_IDX_HEADS = 8
IDX_DIM = 64
TOPK_MAX = 256
ROPE_THETA = 10000.0
D_FF = -(-8 * D_MODEL // (3 * 256)) * 256
IN_DIM = 2 * C_CONV + (N_HEADS + 2 * N_KV_HEADS) * HEAD_DIM + N_IDX_HEADS * IDX_DIM + IDX_DIM + N_IDX_HEADS
DEPTH = 2
ALPHA = (2 * DEPTH) ** 0.25
ATTN_SCALE = HEAD_DIM ** -0.5
IDX_SCALE = IDX_DIM ** -0.5
IDX_HEAD_SCALE = N_IDX_HEADS ** -0.5
LN_EPS = 1e-5

LANES = 128
SUBLANES = 8
VMEM_LIMIT = 56 * 1024 * 1024
IN_PAD = -(-IN_DIM // LANES) * LANES
HALO = 32
NEG_BIG = -1e30
INT_MIN = -(2 ** 31)
F32_LOWEST = float(jnp.finfo(jnp.float32).min)

_O_A = 0
_O_G = _O_A + C_CONV
_O_Q = _O_G + C_CONV
_O_K = _O_Q + ATTN_WIDTH
_O_V = _O_K + N_KV_HEADS * HEAD_DIM
_O_QI = _O_V + N_KV_HEADS * HEAD_DIM
_O_KW = _O_QI + N_IDX_HEADS * IDX_DIM

_NT = (((1,), (1,)), ((), ()))

f32 = jnp.float32
bf16 = jnp.bfloat16


def _layer_norm(y, g, b):
    mu = jnp.mean(y, axis=-1, keepdims=True)
    d = y - mu
    var = jnp.mean(d * d, axis=-1, keepdims=True)
    return d * lax.rsqrt(var + LN_EPS) * g + b


def _rope128(x, cos, sa, sb):
    return x * cos + pltpu.roll(x, 96, 1) * sa + pltpu.roll(x, 32, 1) * sb


def _inproj_kernel(x_ref, w_ref, cos_ref, sa_ref, sb_ref, cosk_ref, sak_ref, sbk_ref,
                   u_ref, q_ref, k_ref, v_ref, qi_ref, kw_ref):
    xb = x_ref[...].astype(bf16)

    def mm(lo, width):
        return jnp.dot(xb, w_ref[:, lo:lo + width], preferred_element_type=f32)

    u_ref[...] = mm(_O_A, C_CONV) * jax.nn.sigmoid(mm(_O_G, C_CONV))

    cos, sa, sb = cos_ref[...], sa_ref[...], sb_ref[...]
    q = mm(_O_Q, ATTN_WIDTH)
    qi = mm(_O_QI, N_IDX_HEADS * IDX_DIM)
    for c in range(ATTN_WIDTH // LANES):
        r = _rope128(q[:, c * LANES:(c + 1) * LANES], cos, sa, sb) * ATTN_SCALE
        q_ref[2 * c] = r[:, :HEAD_DIM].astype(bf16)
        q_ref[2 * c + 1] = r[:, HEAD_DIM:].astype(bf16)
        r = _rope128(qi[:, c * LANES:(c + 1) * LANES], cos, sa, sb)
        qi_ref[2 * c] = r[:, :IDX_DIM].astype(bf16)
        qi_ref[2 * c + 1] = r[:, IDX_DIM:].astype(bf16)
    k_ref[...] = _rope128(mm(_O_K, N_KV_HEADS * HEAD_DIM), cos, sa, sb)
    v_ref[...] = mm(_O_V, N_KV_HEADS * HEAD_DIM)
    kw_ref[...] = _rope128(mm(_O_KW, LANES), cosk_ref[...], sak_ref[...], sbk_ref[...])


def _rope_tables(T, past, rows):
    half = HEAD_DIM // 2
    inv = ROPE_THETA ** (-jnp.arange(half, dtype=f32) / half)
    pos = (jnp.arange(T, dtype=jnp.int32) + past).astype(f32)
    ang = pos[:, None] * inv[None, :]
    cos, sin = jnp.cos(ang), jnp.sin(ang)
    zero = jnp.zeros_like(sin)
    cos64 = jnp.concatenate([cos, cos], axis=1)
    sa64 = jnp.concatenate([-sin, zero], axis=1)
    sb64 = jnp.concatenate([zero, sin], axis=1)
    two = lambda a: jnp.concatenate([a, a], axis=1)
    wscale = jnp.full((T, N_IDX_HEADS), IDX_HEAD_SCALE * IDX_SCALE, f32)
    ones = jnp.ones((T, LANES - IDX_DIM - N_IDX_HEADS), f32)
    cosk = jnp.concatenate([cos64, wscale, ones], axis=1)
    zpad = jnp.zeros((T, LANES - IDX_DIM), f32)
    tabs = (two(cos64), two(sa64), two(sb64), cosk,
            jnp.concatenate([sa64, zpad], axis=1), jnp.concatenate([sb64, zpad], axis=1))
    reps = rows // T
    return tuple(jnp.tile(t, (reps, 1)) for t in tabs)


def _inproj(x2d, w_pad, T, past):
    N = x2d.shape[0]
    tm = min(512, N)
    rows = max(T, tm)
    tabs = _rope_tables(T, past, rows)
    nt = rows // tm
    row_spec = lambda w: pl.BlockSpec((tm, w), lambda i: (i, 0))
    tab_spec = pl.BlockSpec((tm, LANES), lambda i: (i % nt, 0))
    hm_spec = pl.BlockSpec((N_HEADS, tm, HEAD_DIM), lambda i: (0, i, 0))
    return pl.pallas_call(
        _inproj_kernel,
        grid=(N // tm,),
        in_specs=[row_spec(D_MODEL), pl.BlockSpec((D_MODEL, IN_PAD), lambda i: (0, 0))] + [tab_spec] * 6,
        out_specs=[row_spec(C_CONV), hm_spec, row_spec(LANES), row_spec(LANES), hm_spec, row_spec(LANES)],
        out_shape=[
            jax.ShapeDtypeStruct((N, C_CONV), f32),
            jax.ShapeDtypeStruct((N_HEADS, N, HEAD_DIM), bf16),
            jax.ShapeDtypeStruct((N, N_KV_HEADS * HEAD_DIM), f32),
            jax.ShapeDtypeStruct((N, N_KV_HEADS * HEAD_DIM), f32),
            jax.ShapeDtypeStruct((N_IDX_HEADS, N, IDX_DIM), bf16),
            jax.ShapeDtypeStruct((N, LANES), f32),
        ],
        compiler_params=pltpu.CompilerParams(dimension_semantics=("arbitrary",), vmem_limit_bytes=VMEM_LIMIT),
        name="inproj",
    )(x2d, w_pad, *tabs)


def _conv_kernel(u_ref, halo_ref, st_ref, w_ref, b_ref, g_ref, bt_ref, o_ref, f_ref, *, tc, rb):
    i = pl.program_id(1)
    f_ref[0:HALO, :] = jnp.where(i == 0, st_ref[0], halo_ref[0])
    f_ref[HALO:HALO + tc, :] = u_ref[0]
    lead = HALO - (CONV_WIDTH - 1)
    for r in range(tc // rb):
        acc = jnp.zeros((rb, C_CONV), f32)
        for j in range(CONV_WIDTH):
            lo = r * rb + lead + j
            acc = acc + f_ref[lo:lo + rb, :] * w_ref[j:j + 1, :]
        y = _layer_norm(acc + b_ref[...], g_ref[...], bt_ref[...])
        o_ref[0, r * rb:(r + 1) * rb, :] = (y * jax.nn.sigmoid(y)).astype(o_ref.dtype)


def _conv(u3, state, conv_w, conv_b, ln_g, ln_b):
    B, T, _ = u3.shape
    tc = min(256, T)
    st = jnp.pad(state.astype(f32), ((0, 0), (HALO - (CONV_WIDTH - 1), 0), (0, 0)))
    wp = jnp.pad(conv_w, ((0, HALO - CONV_WIDTH), (0, 0)))
    vec = lambda a: a.reshape(1, C_CONV)
    hb = tc // HALO
    return pl.pallas_call(
        functools.partial(_conv_kernel, tc=tc, rb=32),
        grid=(B, T // tc),
        in_specs=[
            pl.BlockSpec((1, tc, C_CONV), lambda b, i: (b, i, 0)),
            pl.BlockSpec((1, HALO, C_CONV), lambda b, i: (b, jnp.maximum(i * hb - 1, 0), 0)),
            pl.BlockSpec((1, HALO, C_CONV), lambda b, i: (b, 0, 0)),
            pl.BlockSpec((HALO, C_CONV), lambda b, i: (0, 0)),
        ] + [pl.BlockSpec((1, C_CONV), lambda b, i: (0, 0))] * 3,
        out_specs=pl.BlockSpec((1, tc, C_CONV), lambda b, i: (b, i, 0)),
        out_shape=jax.ShapeDtypeStruct((B, T, C_CONV), bf16),
        scratch_shapes=[pltpu.VMEM((HALO + tc, C_CONV), f32)],
        compiler_params=pltpu.CompilerParams(dimension_semantics=("arbitrary", "arbitrary"),
                                             vmem_limit_bytes=VMEM_LIMIT),
        name="conv",
    )(u3, u3, st, wp, vec(conv_b), vec(ln_g), vec(ln_b))


def _key_to_f32(key):
    bits = key ^ ((key >> 31) & jnp.int32(0x7FFFFFFF))
    return pltpu.bitcast(bits, f32)


def _attn_kernel(qi_ref, wi_ref, q_ref, ki_ref, k_ref, vt_ref, o_ref, sc_ref, ot_ref,
                 *, tq, ts, t1, past, topk, idx_bits):
    j = pl.program_id(1)
    n_full = (past + j * tq) // ts
    n_chunks = n_full + 1
    kf = float(topk)

    def score_rows(row0, admit):
        kic = ki_ref[0, pl.ds(row0, t1), :]
        acc = jnp.zeros((t1, tq), f32)
        for h in range(N_IDX_HEADS):
            d = lax.dot_general(kic, qi_ref[h], _NT, preferred_element_type=f32)
            acc = acc + jnp.maximum(d, 0.0) * wi_ref[0, h:h + 1, :]
        if admit is not None:
            acc = jnp.where(admit, acc, -jnp.inf)
        sc_ref[pl.ds(row0, t1), :] = acc

    def full_body(i, carry):
        score_rows(pl.multiple_of(i * t1, t1), None)
        return carry

    lax.fori_loop(0, n_full * (ts // t1), full_body, 0)
    q_chunk = lax.broadcasted_iota(jnp.int32, (t1, tq), 1) // CHUNK
    for r in range(ts // t1):
        k_chunk = (lax.broadcasted_iota(jnp.int32, (t1, tq), 0) + r * t1) // CHUNK
        score_rows(pl.multiple_of(n_full * ts + r * t1, t1), k_chunk <= q_chunk)

    def count(preds):
        def body(c, accs):
            row0 = pl.multiple_of(c * ts, ts)
            blk = sc_ref[pl.ds(row0, ts), :]
            out = []
            for pred, acc in zip(preds, accs):
                m = jnp.where(pred(blk, row0), 1.0, 0.0)
                out.append(acc + m.reshape(ts // SUBLANES, SUBLANES, tq).sum(axis=0))
            return tuple(out)
        init = tuple(jnp.zeros((SUBLANES, tq), f32) for _ in preds)
        accs = lax.fori_loop(0, n_chunks, body, init)
        return [a.sum(axis=0, keepdims=True) for a in accs]

    def bisect(it, key):
        cand = key + lax.shift_left(jnp.int32(1), 31 - it)
        thr = _key_to_f32(cand)
        cnt, = count([lambda blk, row0: blk >= thr])
        return jnp.where(cnt >= kf, cand, key)

    key = lax.fori_loop(0, 32, bisect, jnp.full((1, tq), INT_MIN, jnp.int32))
    few = key == INT_MIN
    thr = jnp.where(few, F32_LOWEST, _key_to_f32(key))
    c_ge, c_gt = count([lambda blk, row0: blk >= thr, lambda blk, row0: blk > thr])
    ties = c_ge > kf
    need = kf - c_gt
    any_ties = jnp.max(jnp.where(ties, 1.0, 0.0))

    def write_bias(select):
        def body(c, carry):
            row0 = pl.multiple_of(c * ts, ts)
            blk = sc_ref[pl.ds(row0, ts), :]
            sc_ref[pl.ds(row0, ts), :] = jnp.where(select(blk, row0), 0.0, NEG_BIG)
            return carry
        lax.fori_loop(0, n_chunks, body, 0)

    @pl.when(any_ties == 0.0)
    def _():
        write_bias(lambda blk, row0: blk >= thr)

    @pl.when(any_ties > 0.0)
    def _():
        def rows(row0):
            return lax.broadcasted_iota(jnp.int32, (ts, tq), 0) + row0

        def cut_bisect(it, cut):
            cand = cut + lax.shift_left(jnp.int32(1), idx_bits - 1 - it)
            cnt, = count([lambda blk, row0: jnp.logical_and(blk == thr, rows(row0) < cand)])
            return jnp.where(cnt < need, cand, cut)

        cut = lax.fori_loop(0, idx_bits, cut_bisect, jnp.zeros((1, tq), jnp.int32))
        cut = jnp.where(ties, cut, jnp.int32(2 ** 30))
        write_bias(lambda blk, row0: jnp.logical_or(
            blk > thr, jnp.logical_and(blk == thr, rows(row0) <= cut)))

    for h in range(N_HEADS):
        g = h // GROUP
        qh = q_ref[h]

        def body(c, carry, g=g, qh=qh):
            m, l, acc = carry
            row0 = pl.multiple_of(c * ts, ts)
            kc = k_ref[0, g, pl.ds(row0, ts), :]
            s = lax.dot_general(kc, qh, _NT, preferred_element_type=f32) + sc_ref[pl.ds(row0, ts), :]
            s3 = s.reshape(ts // SUBLANES, SUBLANES, tq)
            m_new = jnp.maximum(m, s3.max(axis=0).max(axis=0, keepdims=True))
            alpha = jnp.exp(m - m_new)
            p = jnp.exp(s - m_new)
            l = alpha * l + p.reshape(ts // SUBLANES, SUBLANES, tq).sum(axis=0).sum(axis=0, keepdims=True)
            vt = vt_ref[0, g, :, pl.ds(row0, ts)]
            acc = alpha * acc + jnp.dot(vt, p.astype(bf16), preferred_element_type=f32)
            return m_new, l, acc

        init = (jnp.full((1, tq), NEG_BIG, f32), jnp.zeros((1, tq), f32), jnp.zeros((HEAD_DIM, tq), f32))
        m, l, acc = lax.fori_loop(0, n_chunks, body, init)
        ot_ref[h * HEAD_DIM:(h + 1) * HEAD_DIM, :] = acc / l
    o_ref[...] = ot_ref[...].T.astype(o_ref.dtype)


def _attention(qi_hm, wi3, q_hm, ki_all, k_hm, vt_hm, *, B, T, past, topk):
    tq = min(256, T)
    ts = 256
    assert past % ts == 0 and (tq == ts or tq == T) and tq % CHUNK == 0
    nq = T // tq
    s_pad = ki_all.shape[1]
    assert s_pad == (past + T + ts - 1) // ts * ts
    N = B * T
    hm_spec = pl.BlockSpec((N_HEADS, tq, HEAD_DIM), lambda b, j: (0, b * nq + j, 0))
    return pl.pallas_call(
        functools.partial(_attn_kernel, tq=tq, ts=ts, t1=128, past=past, topk=topk,
                          idx_bits=max(1, math.ceil(math.log2(s_pad)))),
        grid=(B, nq),
        in_specs=[
            hm_spec,
            pl.BlockSpec((1, N_IDX_HEADS, tq), lambda b, j: (b * nq + j, 0, 0)),
            hm_spec,
            pl.BlockSpec((1, s_pad, IDX_DIM), lambda b, j: (b, 0, 0)),
            pl.BlockSpec((1, N_KV_HEADS, s_pad, HEAD_DIM), lambda b, j: (b, 0, 0, 0)),
            pl.BlockSpec((1, N_KV_HEADS, HEAD_DIM, s_pad), lambda b, j: (b, 0, 0, 0)),
        ],
        out_specs=pl.BlockSpec((tq, ATTN_WIDTH), lambda b, j: (b * nq + j, 0)),
        out_shape=jax.ShapeDtypeStruct((N, ATTN_WIDTH), bf16),
        scratch_shapes=[pltpu.VMEM((s_pad, tq), f32), pltpu.VMEM((ATTN_WIDTH, tq), f32)],
        compiler_params=pltpu.CompilerParams(dimension_semantics=("arbitrary", "arbitrary"),
                                             vmem_limit_bytes=VMEM_LIMIT),
        name="dsa_attention",
    )(qi_hm, wi3, q_hm, ki_all, k_hm, vt_hm)


FF_CHUNK = 256


def _post_kernel(x_ref, c_ref, a_ref, woc_ref, woa_ref, g1_ref, b1_ref, wg_ref, wu_ref, wd_ref,
                 g2_ref, b2_ref, o_ref, acc_ref):
    mix = (jnp.dot(c_ref[...], woc_ref[...], preferred_element_type=f32)
           + jnp.dot(a_ref[...], woa_ref[...], preferred_element_type=f32))
    x1 = _layer_norm(ALPHA * x_ref[...] + mix, g1_ref[...], b1_ref[...])
    x1b = x1.astype(bf16)
    for c in range(D_FF // FF_CHUNK):
        lo = c * FF_CHUNK
        gate = jnp.dot(x1b, wg_ref[:, lo:lo + FF_CHUNK], preferred_element_type=f32)
        up = jnp.dot(x1b, wu_ref[:, lo:lo + FF_CHUNK], preferred_element_type=f32)
        act = (gate * jax.nn.sigmoid(gate) * up).astype(bf16)
        part = jnp.dot(act, wd_ref[lo:lo + FF_CHUNK, :], preferred_element_type=f32)
        if c == 0:
            acc_ref[...] = part
        else:
            acc_ref[...] += part
    o_ref[...] = _layer_norm(ALPHA * x1 + acc_ref[...], g2_ref[...], b2_ref[...])


def _post(x2d, conv2d, attn2d, woc, woa, g1, b1, wg, wu, wd, g2, b2):
    N = x2d.shape[0]
    tm = min(512, N)
    row = lambda w: pl.BlockSpec((tm, w), lambda i: (i, 0))
    const = lambda a: pl.BlockSpec(a.shape, lambda i: (0, 0))
    vec = lambda a: a.reshape(1, D_MODEL)
    args = (x2d, conv2d, attn2d, woc, woa, vec(g1), vec(b1), wg, wu, wd, vec(g2), vec(b2))
    return pl.pallas_call(
        _post_kernel,
        grid=(N // tm,),
        in_specs=[row(D_MODEL), row(C_CONV), row(ATTN_WIDTH)] + [const(a) for a in args[3:]],
        out_specs=row(D_MODEL),
        out_shape=jax.ShapeDtypeStruct((N, D_MODEL), f32),
        scratch_shapes=[pltpu.VMEM((tm, D_MODEL), f32)],
        compiler_params=pltpu.CompilerParams(dimension_semantics=("arbitrary",), vmem_limit_bytes=VMEM_LIMIT),
        name="post",
    )(*args)


def _layer(x, conv_state, k_past, v_past, ki_past, wts):
    (w_pad, conv_w, conv_b, cln_g, cln_b, woc, woa, ln1_g, ln1_b, wg, wu, wd, ln2_g, ln2_b) = wts
    B, T, _ = x.shape
    past = k_past.shape[1]
    N = B * T
    x2d = x.reshape(N, D_MODEL)
    u, q_hm, k2d, v2d, qi_hm, kw = _inproj(x2d, w_pad, T, past)

    u3 = u.reshape(B, T, C_CONV)
    conv_out = _conv(u3, conv_state, conv_w, conv_b, cln_g, cln_b)
    new_conv = jnp.concatenate([conv_state.astype(f32), u3], axis=1)[:, -(CONV_WIDTH - 1):]

    k = k2d.reshape(B, T, N_KV_HEADS, HEAD_DIM)
    v = v2d.reshape(B, T, N_KV_HEADS, HEAD_DIM)
    ki = kw[:, :IDX_DIM].reshape(B, T, IDX_DIM)
    S = past + T
    tq = min(256, T)
    s_pad = -(-S // 256) * 256
    pad_s = lambda a: jnp.pad(a, ((0, 0), (0, s_pad - S)) + ((0, 0),) * (a.ndim - 2))
    k_all = pad_s(jnp.concatenate([k_past.astype(bf16), k.astype(bf16)], axis=1))
    v_all = pad_s(jnp.concatenate([v_past.astype(bf16), v.astype(bf16)], axis=1))
    ki_all = pad_s(jnp.concatenate([ki_past.astype(bf16), ki.astype(bf16)], axis=1))
    k_hm = jnp.transpose(k_all, (0, 2, 1, 3))
    vt_hm = jnp.transpose(v_all, (0, 2, 3, 1))
    wi3 = jnp.transpose(kw[:, IDX_DIM:IDX_DIM + N_IDX_HEADS].reshape(N // tq, tq, N_IDX_HEADS), (0, 2, 1))
    attn = _attention(qi_hm, wi3, q_hm, ki_all, k_hm, vt_hm, B=B, T=T, past=past, topk=min(TOPK_MAX, S // 4))

    y = _post(x2d, conv_out.reshape(N, C_CONV), attn, woc, woa, ln1_g, ln1_b, wg, wu, wd, ln2_g, ln2_b)
    return y.reshape(B, T, D_MODEL), k, v, ki, new_conv


def kernel(x_prompt, x_sample, cache_k, cache_v, cache_k_idx, state_conv, w_in, conv_w, conv_b, conv_ln_g,
           conv_ln_b, w_o, ln1_g, ln1_b, w_gate_up, w_down, ln2_g, ln2_b):
    Bp = x_prompt.shape[0]
    dt = x_prompt.dtype
    empty_kv = jnp.zeros((Bp, 0, N_KV_HEADS, HEAD_DIM), dt)
    empty_ki = jnp.zeros((Bp, 0, IDX_DIM), dt)
    zero_conv = jnp.zeros((Bp, CONV_WIDTH - 1, C_CONV), dt)
    hp, hs = x_prompt, x_sample
    outs_p, outs_s = [], []
    for l in range(DEPTH):
        wts = (
            jnp.pad(w_in[l], ((0, 0), (0, IN_PAD - IN_DIM))).astype(bf16),
            conv_w[l], conv_b[l], conv_ln_g[l], conv_ln_b[l],
            w_o[l, :C_CONV].astype(bf16), w_o[l, C_CONV:].astype(bf16), ln1_g[l], ln1_b[l],
            w_gate_up[l, :, :D_FF].astype(bf16), w_gate_up[l, :, D_FF:].astype(bf16), w_down[l].astype(bf16),
            ln2_g[l], ln2_b[l],
        )
        hp, *op = _layer(hp, zero_conv, empty_kv, empty_kv, empty_ki, wts)
        hs, *os_ = _layer(hs, state_conv[l], cache_k[l], cache_v[l], cache_k_idx[l], wts)
        outs_p.append(op)
        outs_s.append(os_)
    stack = lambda outs, i: jnp.stack([o[i] for o in outs])
    return (hp, hs,
            stack(outs_p, 0), stack(outs_p, 1), stack(outs_p, 2), stack(outs_p, 3),
            stack(outs_s, 0), stack(outs_s, 1), stack(outs_s, 2), stack(outs_s, 3))
```

```python
import functools
import math

import jax
import jax.numpy as jnp
from jax import lax
from jax.experimental import pallas as pl
from jax.experimental.pallas import tpu as pltpu

D_MODEL = 1024
CHUNK = 64
C_CONV = D_MODEL // 2
CONV_WIDTH = 31
HEAD_DIM = 64
N_HEADS = (D_MODEL // 2) // HEAD_DIM
N_KV_HEADS = 2
GROUP = N_HEADS // N_KV_HEADS
ATTN_WIDTH = N_HEADS * HEAD_DIM
N_IDX_HEADS = 8
IDX_DIM = 64
TOPK_MAX = 256
ROPE_THETA = 10000.0
D_FF = -(-8 * D_MODEL // (3 * 256)) * 256
IN_DIM = 2 * C_CONV + (N_HEADS + 2 * N_KV_HEADS) * HEAD_DIM + N_IDX_HEADS * IDX_DIM + IDX_DIM + N_IDX_HEADS
DEPTH = 2
ALPHA = (2 * DEPTH) ** 0.25
ATTN_SCALE = HEAD_DIM ** -0.5
IDX_SCALE = IDX_DIM ** -0.5
IDX_HEAD_SCALE = N_IDX_HEADS ** -0.5
LN_EPS = 1e-5
Q_SCALE = ATTN_SCALE * math.log2(math.e)

LANES = 128
SUBLANES = 8
VMEM_LIMIT = 56 * 1024 * 1024
IN_PAD = -(-IN_DIM // LANES) * LANES
HALO = 32
NEG_BIG = -1e30
INT_MIN = -(2 ** 31)
F32_LOWEST = float(jnp.finfo(jnp.float32).min)

_O_A = 0
_O_G = _O_A + C_CONV
_O_Q = _O_G + C_CONV
_O_K = _O_Q + ATTN_WIDTH
_O_V = _O_K + N_KV_HEADS * HEAD_DIM
_O_QI = _O_V + N_KV_HEADS * HEAD_DIM
_O_KW = _O_QI + N_IDX_HEADS * IDX_DIM

_NT = (((1,), (1,)), ((), ()))

f32 = jnp.float32
bf16 = jnp.bfloat16


def _layer_norm(y, g, b):
    mu = jnp.mean(y, axis=-1, keepdims=True)
    d = y - mu
    var = jnp.mean(d * d, axis=-1, keepdims=True)
    return d * lax.rsqrt(var + LN_EPS) * g + b


def _rope128(x, cos, sa, sb):
    return x * cos + pltpu.roll(x, 96, 1) * sa + pltpu.roll(x, 32, 1) * sb


def _inproj_kernel(x_ref, w_ref, cos_ref, sa_ref, sb_ref, cosk_ref, sak_ref, sbk_ref,
                   u_ref, q_ref, k_ref, v_ref, qi_ref, kw_ref):
    xb = x_ref[...].astype(bf16)

    def mm(lo, width):
        return jnp.dot(xb, w_ref[:, lo:lo + width], preferred_element_type=f32)

    u_ref[...] = mm(_O_A, C_CONV) * jax.nn.sigmoid(mm(_O_G, C_CONV))

    cos, sa, sb = cos_ref[...], sa_ref[...], sb_ref[...]
    q = mm(_O_Q, ATTN_WIDTH)
    qi = mm(_O_QI, N_IDX_HEADS * IDX_DIM)
    for c in range(ATTN_WIDTH // LANES):
        r = _rope128(q[:, c * LANES:(c + 1) * LANES], cos, sa, sb) * Q_SCALE
        q_ref[2 * c] = r[:, :HEAD_DIM].astype(bf16)
        q_ref[2 * c + 1] = r[:, HEAD_DIM:].astype(bf16)
        r = _rope128(qi[:, c * LANES:(c + 1) * LANES], cos, sa, sb)
        qi_ref[2 * c] = r[:, :IDX_DIM].astype(bf16)
        qi_ref[2 * c + 1] = r[:, IDX_DIM:].astype(bf16)
    k_ref[...] = _rope128(mm(_O_K, N_KV_HEADS * HEAD_DIM), cos, sa, sb)
    v_ref[...] = mm(_O_V, N_KV_HEADS * HEAD_DIM)
    kw_ref[...] = _rope128(mm(_O_KW, LANES), cosk_ref[...], sak_ref[...], sbk_ref[...])


def _rope_tables(T, past, rows):
    half = HEAD_DIM // 2
    inv = ROPE_THETA ** (-jnp.arange(half, dtype=f32) / half)
    pos = (jnp.arange(T, dtype=jnp.int32) + past).astype(f32)
    ang = pos[:, None] * inv[None, :]
    cos, sin = jnp.cos(ang), jnp.sin(ang)
    zero = jnp.zeros_like(sin)
    cos64 = jnp.concatenate([cos, cos], axis=1)
    sa64 = jnp.concatenate([-sin, zero], axis=1)
    sb64 = jnp.concatenate([zero, sin], axis=1)
    two = lambda a: jnp.concatenate([a, a], axis=1)
    wscale = jnp.full((T, N_IDX_HEADS), IDX_HEAD_SCALE * IDX_SCALE, f32)
    ones = jnp.ones((T, LANES - IDX_DIM - N_IDX_HEADS), f32)
    cosk = jnp.concatenate([cos64, wscale, ones], axis=1)
    zpad = jnp.zeros((T, LANES - IDX_DIM), f32)
    tabs = (two(cos64), two(sa64), two(sb64), cosk,
            jnp.concatenate([sa64, zpad], axis=1), jnp.concatenate([sb64, zpad], axis=1))
    reps = rows // T
    return tuple(jnp.tile(t, (reps, 1)) for t in tabs)


def _inproj(x2d, w_pad, T, past):
    N = x2d.shape[0]
    tm = min(512, N)
    rows = max(T, tm)
    tabs = _rope_tables(T, past, rows)
    nt = rows // tm
    row_spec = lambda w: pl.BlockSpec((tm, w), lambda i: (i, 0))
    tab_spec = pl.BlockSpec((tm, LANES), lambda i: (i % nt, 0))
    hm_spec = pl.BlockSpec((N_HEADS, tm, HEAD_DIM), lambda i: (0, i, 0))
    return pl.pallas_call(
        _inproj_kernel,
        grid=(N // tm,),
        in_specs=[row_spec(D_MODEL), pl.BlockSpec((D_MODEL, IN_PAD), lambda i: (0, 0))] + [tab_spec] * 6,
        out_specs=[row_spec(C_CONV), hm_spec, row_spec(LANES), row_spec(LANES), hm_spec, row_spec(LANES)],
        out_shape=[
            jax.ShapeDtypeStruct((N, C_CONV), f32),
            jax.ShapeDtypeStruct((N_HEADS, N, HEAD_DIM), bf16),
            jax.ShapeDtypeStruct((N, N_KV_HEADS * HEAD_DIM), f32),
            jax.ShapeDtypeStruct((N, N_KV_HEADS * HEAD_DIM), f32),
            jax.ShapeDtypeStruct((N_IDX_HEADS, N, IDX_DIM), bf16),
            jax.ShapeDtypeStruct((N, LANES), f32),
        ],
        compiler_params=pltpu.CompilerParams(dimension_semantics=("arbitrary",), vmem_limit_bytes=VMEM_LIMIT),
        name="inproj",
    )(x2d, w_pad, *tabs)


def _conv_kernel(u_ref, halo_ref, st_ref, w_ref, b_ref, g_ref, bt_ref, o_ref, f_ref, *, tc, rb):
    i = pl.program_id(1)
    f_ref[0:HALO, :] = jnp.where(i == 0, st_ref[0], halo_ref[0])
    f_ref[HALO:HALO + tc, :] = u_ref[0]
    lead = HALO - (CONV_WIDTH - 1)
    for r in range(tc // rb):
        acc = jnp.zeros((rb, C_CONV), f32)
        for j in range(CONV_WIDTH):
            lo = r * rb + lead + j
            acc = acc + f_ref[lo:lo + rb, :] * w_ref[j:j + 1, :]
        y = _layer_norm(acc + b_ref[...], g_ref[...], bt_ref[...])
        o_ref[0, r * rb:(r + 1) * rb, :] = (y * jax.nn.sigmoid(y)).astype(o_ref.dtype)


def _conv(u3, state, conv_w, conv_b, ln_g, ln_b):
    B, T, _ = u3.shape
    tc = min(256, T)
    st = jnp.pad(state.astype(f32), ((0, 0), (HALO - (CONV_WIDTH - 1), 0), (0, 0)))
    wp = jnp.pad(conv_w, ((0, HALO - CONV_WIDTH), (0, 0)))
    vec = lambda a: a.reshape(1, C_CONV)
    hb = tc // HALO
    return pl.pallas_call(
        functools.partial(_conv_kernel, tc=tc, rb=32),
        grid=(B, T // tc),
        in_specs=[
            pl.BlockSpec((1, tc, C_CONV), lambda b, i: (b, i, 0)),
            pl.BlockSpec((1, HALO, C_CONV), lambda b, i: (b, jnp.maximum(i * hb - 1, 0), 0)),
            pl.BlockSpec((1, HALO, C_CONV), lambda b, i: (b, 0, 0)),
            pl.BlockSpec((HALO, C_CONV), lambda b, i: (0, 0)),
        ] + [pl.BlockSpec((1, C_CONV), lambda b, i: (0, 0))] * 3,
        out_specs=pl.BlockSpec((1, tc, C_CONV), lambda b, i: (b, i, 0)),
        out_shape=jax.ShapeDtypeStruct((B, T, C_CONV), bf16),
        scratch_shapes=[pltpu.VMEM((HALO + tc, C_CONV), f32)],
        compiler_params=pltpu.CompilerParams(dimension_semantics=("arbitrary", "arbitrary"),
                                             vmem_limit_bytes=VMEM_LIMIT),
        name="conv",
    )(u3, u3, st, wp, vec(conv_b), vec(ln_g), vec(ln_b))


def _key_to_f32(key):
    bits = key ^ ((key >> 31) & jnp.int32(0x7FFFFFFF))
    return pltpu.bitcast(bits, f32)


def _attn_kernel(qi_ref, wi_ref, q_ref, ki_ref, k_ref, vt_ref, o_ref, sc_ref, m_ref, acc_ref, ot_ref, s_ref,
                 *, tq, hp, ts, t1, ta, past, topk, idx_bits):
    nvh = N_HEADS // hp
    lw = hp * tq
    lw2 = sc_ref.shape[1]
    j = pl.program_id(1)
    n_full = (past + j * tq) // ts
    n_chunks = n_full + 1
    kf = float(topk)

    q_chunk = (lax.broadcasted_iota(jnp.int32, (t1, lw2), 1) % tq) // CHUNK

    def score_chunk(c, masked):
        for r in range(ts // t1):
            row0 = pl.multiple_of(c * ts + r * t1, t1)
            kic = ki_ref[0, pl.ds(row0, t1), :]
            acc = jnp.zeros((t1, lw), f32)
            for hv in range(nvh):
                d = lax.dot_general(kic, qi_ref[hv], _NT, preferred_element_type=f32)
                acc = acc + jnp.maximum(d, 0.0) * wi_ref[0, hv:hv + 1, :]
            width = lw
            while width > tq:
                width //= 2
                acc = acc + pltpu.roll(acc, width, 1)
            acc = acc[:, :lw2]
            if masked:
                k_chunk = (lax.broadcasted_iota(jnp.int32, (t1, lw2), 0) + r * t1) // CHUNK
                acc = jnp.where(k_chunk <= q_chunk, acc, -jnp.inf)
            sc_ref[pl.ds(row0, t1), :] = acc

    def pair_body(i, carry):
        score_chunk(2 * i, False)
        score_chunk(2 * i + 1, False)
        return carry

    lax.fori_loop(0, n_full // 2, pair_body, 0)

    @pl.when(n_full % 2 == 1)
    def _():
        score_chunk(n_full - 1, False)

    score_chunk(n_full, True)

    def count(preds):
        def body(c, accs):
            row0 = pl.multiple_of(c * ts, ts)
            blk = sc_ref[pl.ds(row0, ts), :]
            out = []
            for pred, acc in zip(preds, accs):
                m = jnp.where(pred(blk, row0), 1.0, 0.0)
                out.append(acc + m.reshape(ts // SUBLANES, SUBLANES, lw2).sum(axis=0))
            return tuple(out)
        init = tuple(jnp.zeros((SUBLANES, lw2), f32) for _ in preds)
        accs = lax.fori_loop(0, n_chunks, body, init)
        return [a.sum(axis=0, keepdims=True) for a in accs]

    def bisect(it, key):
        cand = key + lax.shift_left(jnp.int32(1), 31 - it)
        thr = _key_to_f32(cand)
        cnt, = count([lambda blk, row0: blk >= thr])
        return jnp.where(cnt >= kf, cand, key)

    key = lax.fori_loop(0, 32, bisect, jnp.full((1, lw2), INT_MIN, jnp.int32))
    few = key == INT_MIN
    thr = jnp.where(few, F32_LOWEST, _key_to_f32(key))
    c_ge, c_gt = count([lambda blk, row0: blk >= thr, lambda blk, row0: blk > thr])
    ties = c_ge > kf
    need = kf - c_gt
    any_ties = jnp.max(jnp.where(ties, 1.0, 0.0))

    def write_bias(select):
        def body(c, carry):
            row0 = pl.multiple_of(c * ts, ts)
            blk = sc_ref[pl.ds(row0, ts), :]
            sc_ref[pl.ds(row0, ts), :] = jnp.where(select(blk, row0), 0.0, NEG_BIG)
            return carry
        lax.fori_loop(0, n_chunks, body, 0)

    @pl.when(any_ties == 0.0)
    def _():
        write_bias(lambda blk, row0: blk >= thr)

    @pl.when(any_ties > 0.0)
    def _():
        def rows(row0):
            return lax.broadcasted_iota(jnp.int32, (ts, lw2), 0) + row0

        def cut_bisect(it, cut):
            cand = cut + lax.shift_left(jnp.int32(1), idx_bits - 1 - it)
            cnt, = count([lambda blk, row0: jnp.logical_and(blk == thr, rows(row0) < cand)])
            return jnp.where(cnt < need, cand, cut)

        cut = lax.fori_loop(0, idx_bits, cut_bisect, jnp.zeros((1, lw2), jnp.int32))
        cut = jnp.where(ties, cut, jnp.int32(2 ** 30))
        write_bias(lambda blk, row0: jnp.logical_or(
            blk > thr, jnp.logical_and(blk == thr, rows(row0) <= cut)))

    m_ref[...] = jnp.full(m_ref.shape, NEG_BIG, f32)
    acc_ref[...] = jnp.zeros(acc_ref.shape, f32)

    def att_body(c, carry):
        row0 = pl.multiple_of(c * ta, ta)
        mx = []
        for hv in range(nvh):
            g = (hv * hp) // GROUP
            bias = sc_ref[pl.ds(row0, ta), :]
            if lw2 < lw:
                bias = jnp.concatenate([bias] * (lw // lw2), axis=1)
            kc = k_ref[0, g, pl.ds(row0, ta), :]
            s = lax.dot_general(kc, q_ref[hv], _NT, preferred_element_type=f32) + bias
            s_ref[hv] = s
            mx.append(s.reshape(ta // SUBLANES, SUBLANES, lw).max(axis=0).max(axis=0, keepdims=True))
        for hv in range(nvh):
            g = (hv * hp) // GROUP
            m_old = m_ref[hv]
            m_new = jnp.maximum(m_old, mx[hv])
            alpha = jnp.exp2(m_old - m_new)
            p = jnp.exp2(s_ref[hv] - m_new).astype(bf16)
            vt = vt_ref[0, g, :, pl.ds(row0, ta)]
            acc_ref[hv] = alpha * acc_ref[hv] + jnp.dot(vt, p, preferred_element_type=f32)
            m_ref[hv] = m_new
        return carry

    lax.fori_loop(0, n_chunks * (ts // ta), att_body, 0)
    for hv in range(nvh):
        o = acc_ref[hv]
        ot_ref[hv * HEAD_DIM:(hv + 1) * HEAD_DIM, :] = o[:HEAD_DIM] / o[HEAD_DIM:HEAD_DIM + 1]
    if hp == 1:
        o_ref[...] = ot_ref[...].T.astype(o_ref.dtype)
    else:
        o_ref[0] = ot_ref[...].astype(o_ref.dtype)


V_ROWS = HEAD_DIM + 16


def _attention(qi_hm, wi2d, q_hm, ki_all, k_hm, vt_aug, *, B, T, past, topk):
    tq = min(256, T)
    ts = 256
    ta = 256
    hp = ts // tq
    assert past % ts == 0 and (tq == ts or tq == T) and tq % CHUNK == 0 and hp in (1, 2, 4)
    nvh = N_HEADS // hp
    lw = hp * tq
    lw2 = max(tq, LANES)
    nq = T // tq
    nb = B * nq
    s_pad = ki_all.shape[1]
    assert s_pad == (past + T + ts - 1) // ts * ts
    N = B * T

    def pack(a):
        a = a.reshape(nvh, hp, nb, tq, HEAD_DIM)
        return jnp.transpose(a, (0, 2, 1, 3, 4)).reshape(nvh, nb * lw, HEAD_DIM)

    wi3 = jnp.transpose(wi2d.reshape(nb, tq, nvh, hp), (0, 2, 3, 1)).reshape(nb, nvh, lw)
    hm_spec = pl.BlockSpec((nvh, lw, HEAD_DIM), lambda b, j: (0, b * nq + j, 0))
    if hp == 1:
        out_spec = pl.BlockSpec((tq, ATTN_WIDTH), lambda b, j: (b * nq + j, 0))
        out_shape = jax.ShapeDtypeStruct((N, ATTN_WIDTH), bf16)
    else:
        out_spec = pl.BlockSpec((1, nvh * HEAD_DIM, lw), lambda b, j: (b * nq + j, 0, 0))
        out_shape = jax.ShapeDtypeStruct((nb, nvh * HEAD_DIM, lw), bf16)
    out = pl.pallas_call(
        functools.partial(_attn_kernel, tq=tq, hp=hp, ts=ts, t1=128, ta=ta, past=past, topk=topk,
                          idx_bits=max(1, math.ceil(math.log2(s_pad)))),
        grid=(B, nq),
        in_specs=[
            hm_spec,
            pl.BlockSpec((1, nvh, lw), lambda b, j: (b * nq + j, 0, 0)),
            hm_spec,
            pl.BlockSpec((1, s_pad, IDX_DIM), lambda b, j: (b, 0, 0)),
            pl.BlockSpec((1, N_KV_HEADS, s_pad, HEAD_DIM), lambda b, j: (b, 0, 0, 0)),
            pl.BlockSpec((1, N_KV_HEADS, V_ROWS, s_pad), lambda b, j: (b, 0, 0, 0)),
        ],
        out_specs=out_spec,
        out_shape=out_shape,
        scratch_shapes=[pltpu.VMEM((s_pad, lw2), f32), pltpu.VMEM((nvh, 1, lw), f32),
                        pltpu.VMEM((nvh, V_ROWS, lw), f32), pltpu.VMEM((nvh * HEAD_DIM, lw), f32),
                        pltpu.VMEM((nvh, ta, lw), f32)],
        compiler_params=pltpu.CompilerParams(dimension_semantics=("arbitrary", "arbitrary"),
                                             vmem_limit_bytes=VMEM_LIMIT),
        name="dsa_attention",
    )(pack(qi_hm), wi3, pack(q_hm), ki_all, k_hm, vt_aug)
    if hp > 1:
        out = out.reshape(nb, nvh, HEAD_DIM, hp, tq)
        out = jnp.transpose(out, (0, 4, 1, 3, 2)).reshape(N, ATTN_WIDTH)
    return out


FF_CHUNK = 256


def _post_kernel(x_ref, c_ref, a_ref, woc_ref, woa_ref, g1_ref, b1_ref, wg_ref, wu_ref, wd_ref,
                 g2_ref, b2_ref, o_ref, acc_ref):
    mix = (jnp.dot(c_ref[...], woc_ref[...], preferred_element_type=f32)
           + jnp.dot(a_ref[...], woa_ref[...], preferred_element_type=f32))
    x1 = _layer_norm(ALPHA * x_ref[...] + mix, g1_ref[...], b1_ref[...])
    x1b = x1.astype(bf16)
    for c in range(D_FF // FF_CHUNK):
        lo = c * FF_CHUNK
        gate = jnp.dot(x1b, wg_ref[:, lo:lo + FF_CHUNK], preferred_element_type=f32)
        up = jnp.dot(x1b, wu_ref[:, lo:lo + FF_CHUNK], preferred_element_type=f32)
        act = (gate * jax.nn.sigmoid(gate) * up).astype(bf16)
        part = jnp.dot(act, wd_ref[lo:lo + FF_CHUNK, :], preferred_element_type=f32)
        if c == 0:
            acc_ref[...] = part
        else:
            acc_ref[...] += part
    o_ref[...] = _layer_norm(ALPHA * x1 + acc_ref[...], g2_ref[...], b2_ref[...])


def _post(x2d, conv2d, attn2d, woc, woa, g1, b1, wg, wu, wd, g2, b2):
    N = x2d.shape[0]
    tm = min(512, N)
    row = lambda w: pl.BlockSpec((tm, w), lambda i: (i, 0))
    const = lambda a: pl.BlockSpec(a.shape, lambda i: (0, 0))
    vec = lambda a: a.reshape(1, D_MODEL)
    args = (x2d, conv2d, attn2d, woc, woa, vec(g1), vec(b1), wg, wu, wd, vec(g2), vec(b2))
    return pl.pallas_call(
        _post_kernel,
        grid=(N // tm,),
        in_specs=[row(D_MODEL), row(C_CONV), row(ATTN_WIDTH)] + [const(a) for a in args[3:]],
        out_specs=row(D_MODEL),
        out_shape=jax.ShapeDtypeStruct((N, D_MODEL), f32),
        scratch_shapes=[pltpu.VMEM((tm, D_MODEL), f32)],
        compiler_params=pltpu.CompilerParams(dimension_semantics=("arbitrary",), vmem_limit_bytes=VMEM_LIMIT),
        name="post",
    )(*args)


def _layer(x, conv_state, k_past, v_past, ki_past, wts):
    (w_pad, conv_w, conv_b, cln_g, cln_b, woc, woa, ln1_g, ln1_b, wg, wu, wd, ln2_g, ln2_b) = wts
    B, T, _ = x.shape
    past = k_past.shape[1]
    N = B * T
    x2d = x.reshape(N, D_MODEL)
    u, q_hm, k2d, v2d, qi_hm, kw = _inproj(x2d, w_pad, T, past)

    u3 = u.reshape(B, T, C_CONV)
    conv_out = _conv(u3, conv_state, conv_w, conv_b, cln_g, cln_b)
    new_conv = jnp.concatenate([conv_state.astype(f32), u3], axis=1)[:, -(CONV_WIDTH - 1):]

    k = k2d.reshape(B, T, N_KV_HEADS, HEAD_DIM)
    v = v2d.reshape(B, T, N_KV_HEADS, HEAD_DIM)
    ki = kw[:, :IDX_DIM].reshape(B, T, IDX_DIM)
    S = past + T
    s_pad = -(-S // 256) * 256
    pad_s = lambda a: jnp.pad(a, ((0, 0), (0, s_pad - S)) + ((0, 0),) * (a.ndim - 2))
    k_all = pad_s(jnp.concatenate([k_past.astype(bf16), k.astype(bf16)], axis=1))
    v_all = pad_s(jnp.concatenate([v_past.astype(bf16), v.astype(bf16)], axis=1))
    ki_all = pad_s(jnp.concatenate([ki_past.astype(bf16), ki.astype(bf16)], axis=1))
    k_hm = jnp.transpose(k_all, (0, 2, 1, 3))
    vt_hm = jnp.transpose(v_all, (0, 2, 3, 1))
    ones_rows = jnp.concatenate([jnp.ones((B, N_KV_HEADS, 1, s_pad), bf16),
                                 jnp.zeros((B, N_KV_HEADS, V_ROWS - HEAD_DIM - 1, s_pad), bf16)], axis=2)
    vt_aug = jnp.concatenate([vt_hm, ones_rows], axis=2)
    attn = _attention(qi_hm, kw[:, IDX_DIM:IDX_DIM + N_IDX_HEADS], q_hm, ki_all, k_hm, vt_aug,
                      B=B, T=T, past=past, topk=min(TOPK_MAX, S // 4))

    y = _post(x2d, conv_out.reshape(N, C_CONV), attn, woc, woa, ln1_g, ln1_b, wg, wu, wd, ln2_g, ln2_b)
    return y.reshape(B, T, D_MODEL), k, v, ki, new_conv


def kernel(x_prompt, x_sample, cache_k, cache_v, cache_k_idx, state_conv, w_in, conv_w, conv_b, conv_ln_g,
           conv_ln_b, w_o, ln1_g, ln1_b, w_gate_up, w_down, ln2_g, ln2_b):
    Bp = x_prompt.shape[0]
    dt = x_prompt.dtype
    empty_kv = jnp.zeros((Bp, 0, N_KV_HEADS, HEAD_DIM), dt)
    empty_ki = jnp.zeros((Bp, 0, IDX_DIM), dt)
    zero_conv = jnp.zeros((Bp, CONV_WIDTH - 1, C_CONV), dt)
    hp, hs = x_prompt, x_sample
    outs_p, outs_s = [], []
    for l in range(DEPTH):
        wts = (
            jnp.pad(w_in[l], ((0, 0), (0, IN_PAD - IN_DIM))).astype(bf16),
            conv_w[l], conv_b[l], conv_ln_g[l], conv_ln_b[l],
            w_o[l, :C_CONV].astype(bf16), w_o[l, C_CONV:].astype(bf16), ln1_g[l], ln1_b[l],
            w_gate_up[l, :, :D_FF].astype(bf16), w_gate_up[l, :, D_FF:].astype(bf16), w_down[l].astype(bf16),
            ln2_g[l], ln2_b[l],
        )
        hp, *op = _layer(hp, zero_conv, empty_kv, empty_kv, empty_ki, wts)
        hs, *os_ = _layer(hs, state_conv[l], cache_k[l], cache_v[l], cache_k_idx[l], wts)
        outs_p.append(op)
        outs_s.append(os_)
    stack = lambda outs, i: jnp.stack([o[i] for o in outs])
    return (hp, hs,
            stack(outs_p, 0), stack(outs_p, 1), stack(outs_p, 2), stack(outs_p, 3),
            stack(outs_s, 0), stack(outs_s, 1), stack(outs_s, 2), stack(outs_s, 3))
```

```python
import functools
import math

import jax
import jax.numpy as jnp
from jax import lax
from jax.experimental import pallas as pl
from jax.experimental.pallas import tpu as pltpu

D_MODEL = 1024
CHUNK = 64
C_CONV = D_MODEL // 2
CONV_WIDTH = 31
HEAD_DIM = 64
N_HEADS = (D_MODEL // 2) // HEAD_DIM
N_KV_HEADS = 2
GROUP = N_HEADS // N_KV_HEADS
ATTN_WIDTH = N_HEADS * HEAD_DIM
N_IDX_HEADS = 8
IDX_DIM = 64
TOPK_MAX = 256
ROPE_THETA = 10000.0
D_FF = -(-8 * D_MODEL // (3 * 256)) * 256
IN_DIM = 2 * C_CONV + (N_HEADS + 2 * N_KV_HEADS) * HEAD_DIM + N_IDX_HEADS * IDX_DIM + IDX_DIM + N_IDX_HEADS
DEPTH = 2
ALPHA = (2 * DEPTH) ** 0.25
ATTN_SCALE = HEAD_DIM ** -0.5
IDX_SCALE = IDX_DIM ** -0.5
IDX_HEAD_SCALE = N_IDX_HEADS ** -0.5
LN_EPS = 1e-5
Q_SCALE = ATTN_SCALE * math.log2(math.e)

LANES = 128
SUBLANES = 8
VMEM_LIMIT = 56 * 1024 * 1024
IN_PAD = -(-IN_DIM // LANES) * LANES
HALO = 32
NEG_BIG = -1e30
INT_MIN = -(2 ** 31)
F32_LOWEST = float(jnp.finfo(jnp.float32).min)
NEG_INF_WORD = 0x007FFFFF

_O_A = 0
_O_G = _O_A + C_CONV
_O_Q = _O_G + C_CONV
_O_K = _O_Q + ATTN_WIDTH
_O_V = _O_K + N_KV_HEADS * HEAD_DIM
_O_QI = _O_V + N_KV_HEADS * HEAD_DIM
_O_KW = _O_QI + N_IDX_HEADS * IDX_DIM

_NT = (((1,), (1,)), ((), ()))

f32 = jnp.float32
bf16 = jnp.bfloat16


def _layer_norm(y, g, b):
    mu = jnp.mean(y, axis=-1, keepdims=True)
    d = y - mu
    var = jnp.mean(d * d, axis=-1, keepdims=True)
    return d * lax.rsqrt(var + LN_EPS) * g + b


def _rope128(x, cos, sa, sb):
    return x * cos + pltpu.roll(x, 96, 1) * sa + pltpu.roll(x, 32, 1) * sb


def _inproj_kernel(x_ref, w_ref, cos_ref, sa_ref, sb_ref, cosk_ref, sak_ref, sbk_ref,
                   u_ref, q_ref, k_ref, v_ref, qi_ref, kw_ref):
    xb = x_ref[...].astype(bf16)

    def mm(lo, width):
        return jnp.dot(xb, w_ref[:, lo:lo + width], preferred_element_type=f32)

    u_ref[...] = mm(_O_A, C_CONV) * jax.nn.sigmoid(mm(_O_G, C_CONV))

    cos, sa, sb = cos_ref[...], sa_ref[...], sb_ref[...]
    q = mm(_O_Q, ATTN_WIDTH)
    qi = mm(_O_QI, N_IDX_HEADS * IDX_DIM)
    for c in range(ATTN_WIDTH // LANES):
        r = _rope128(q[:, c * LANES:(c + 1) * LANES], cos, sa, sb) * Q_SCALE
        q_ref[2 * c] = r[:, :HEAD_DIM].astype(bf16)
        q_ref[2 * c + 1] = r[:, HEAD_DIM:].astype(bf16)
        r = _rope128(qi[:, c * LANES:(c + 1) * LANES], cos, sa, sb)
        qi_ref[2 * c] = r[:, :IDX_DIM].astype(bf16)
        qi_ref[2 * c + 1] = r[:, IDX_DIM:].astype(bf16)
    k_ref[...] = _rope128(mm(_O_K, N_KV_HEADS * HEAD_DIM), cos, sa, sb)
    v_ref[...] = mm(_O_V, N_KV_HEADS * HEAD_DIM)
    kw_ref[...] = _rope128(mm(_O_KW, LANES), cosk_ref[...], sak_ref[...], sbk_ref[...])


def _rope_tables(T, past, rows):
    half = HEAD_DIM // 2
    inv = ROPE_THETA ** (-jnp.arange(half, dtype=f32) / half)
    pos = (jnp.arange(T, dtype=jnp.int32) + past).astype(f32)
    ang = pos[:, None] * inv[None, :]
    cos, sin = jnp.cos(ang), jnp.sin(ang)
    zero = jnp.zeros_like(sin)
    cos64 = jnp.concatenate([cos, cos], axis=1)
    sa64 = jnp.concatenate([-sin, zero], axis=1)
    sb64 = jnp.concatenate([zero, sin], axis=1)
    two = lambda a: jnp.concatenate([a, a], axis=1)
    wscale = jnp.full((T, N_IDX_HEADS), IDX_HEAD_SCALE * IDX_SCALE, f32)
    ones = jnp.ones((T, LANES - IDX_DIM - N_IDX_HEADS), f32)
    cosk = jnp.concatenate([cos64, wscale, ones], axis=1)
    zpad = jnp.zeros((T, LANES - IDX_DIM), f32)
    tabs = (two(cos64), two(sa64), two(sb64), cosk,
            jnp.concatenate([sa64, zpad], axis=1), jnp.concatenate([sb64, zpad], axis=1))
    reps = rows // T
    return tuple(jnp.tile(t, (reps, 1)) for t in tabs)


def _inproj(x2d, w_pad, T, past):
    N = x2d.shape[0]
    tm = min(512, N)
    rows = max(T, tm)
    tabs = _rope_tables(T, past, rows)
    nt = rows // tm
    row_spec = lambda w: pl.BlockSpec((tm, w), lambda i: (i, 0))
    tab_spec = pl.BlockSpec((tm, LANES), lambda i: (i % nt, 0))
    hm_spec = pl.BlockSpec((N_HEADS, tm, HEAD_DIM), lambda i: (0, i, 0))
    return pl.pallas_call(
        _inproj_kernel,
        grid=(N // tm,),
        in_specs=[row_spec(D_MODEL), pl.BlockSpec((D_MODEL, IN_PAD), lambda i: (0, 0))] + [tab_spec] * 6,
        out_specs=[row_spec(C_CONV), hm_spec, row_spec(LANES), row_spec(LANES), hm_spec, row_spec(LANES)],
        out_shape=[
            jax.ShapeDtypeStruct((N, C_CONV), f32),
            jax.ShapeDtypeStruct((N_HEADS, N, HEAD_DIM), bf16),
            jax.ShapeDtypeStruct((N, N_KV_HEADS * HEAD_DIM), f32),
            jax.ShapeDtypeStruct((N, N_KV_HEADS * HEAD_DIM), f32),
            jax.ShapeDtypeStruct((N_IDX_HEADS, N, IDX_DIM), bf16),
            jax.ShapeDtypeStruct((N, LANES), f32),
        ],
        compiler_params=pltpu.CompilerParams(dimension_semantics=("arbitrary",), vmem_limit_bytes=VMEM_LIMIT),
        name="inproj",
    )(x2d, w_pad, *tabs)


def _conv_kernel(u_ref, halo_ref, st_ref, w_ref, b_ref, g_ref, bt_ref, o_ref, f_ref, *, tc, rb):
    i = pl.program_id(1)
    f_ref[0:HALO, :] = jnp.where(i == 0, st_ref[0], halo_ref[0])
    f_ref[HALO:HALO + tc, :] = u_ref[0]
    lead = HALO - (CONV_WIDTH - 1)
    for r in range(tc // rb):
        acc = jnp.zeros((rb, C_CONV), f32)
        for j in range(CONV_WIDTH):
            lo = r * rb + lead + j
            acc = acc + f_ref[lo:lo + rb, :] * w_ref[j:j + 1, :]
        y = _layer_norm(acc + b_ref[...], g_ref[...], bt_ref[...])
        o_ref[0, r * rb:(r + 1) * rb, :] = (y * jax.nn.sigmoid(y)).astype(o_ref.dtype)


def _conv(u3, state, conv_w, conv_b, ln_g, ln_b):
    B, T, _ = u3.shape
    tc = min(256, T)
    st = jnp.pad(state.astype(f32), ((0, 0), (HALO - (CONV_WIDTH - 1), 0), (0, 0)))
    wp = jnp.pad(conv_w, ((0, HALO - CONV_WIDTH), (0, 0)))
    vec = lambda a: a.reshape(1, C_CONV)
    hb = tc // HALO
    return pl.pallas_call(
        functools.partial(_conv_kernel, tc=tc, rb=32),
        grid=(B, T // tc),
        in_specs=[
            pl.BlockSpec((1, tc, C_CONV), lambda b, i: (b, i, 0)),
            pl.BlockSpec((1, HALO, C_CONV), lambda b, i: (b, jnp.maximum(i * hb - 1, 0), 0)),
            pl.BlockSpec((1, HALO, C_CONV), lambda b, i: (b, 0, 0)),
            pl.BlockSpec((HALO, C_CONV), lambda b, i: (0, 0)),
        ] + [pl.BlockSpec((1, C_CONV), lambda b, i: (0, 0))] * 3,
        out_specs=pl.BlockSpec((1, tc, C_CONV), lambda b, i: (b, i, 0)),
        out_shape=jax.ShapeDtypeStruct((B, T, C_CONV), bf16),
        scratch_shapes=[pltpu.VMEM((HALO + tc, C_CONV), f32)],
        compiler_params=pltpu.CompilerParams(dimension_semantics=("arbitrary", "arbitrary"),
                                             vmem_limit_bytes=VMEM_LIMIT),
        name="conv",
    )(u3, u3, st, wp, vec(conv_b), vec(ln_g), vec(ln_b))


def _key_to_f32(key):
    bits = key ^ ((key >> 31) & jnp.int32(0x7FFFFFFF))
    return pltpu.bitcast(bits, f32)


def _attn_kernel(qi_ref, wi_ref, q_ref, ki_ref, k_ref, vt_ref, o_ref, sc_ref, m_ref, acc_ref, ot_ref, s_ref,
                 st_ref, planes_ref, cand_ref,
                 *, tq, hp, ts, t1, ta, past, topk, idx_bits):
    nvh = N_HEADS // hp
    lw = hp * tq
    lw2 = sc_ref.shape[1]
    j = pl.program_id(1)
    n_full = (past + j * tq) // ts
    n_chunks = n_full + 1
    kf = float(topk)

    q_chunk = (lax.broadcasted_iota(jnp.int32, (t1, lw2), 1) % tq) // CHUNK

    def score_chunk(c, masked):
        for r in range(ts // t1):
            row0 = pl.multiple_of(c * ts + r * t1, t1)
            kic = ki_ref[0, pl.ds(row0, t1), :]
            acc = jnp.zeros((t1, lw), f32)
            for hv in range(nvh):
                d = lax.dot_general(kic, qi_ref[hv], _NT, preferred_element_type=f32)
                acc = acc + jnp.maximum(d, 0.0) * wi_ref[0, hv:hv + 1, :]
            width = lw
            while width > tq:
                width //= 2
                acc = acc + pltpu.roll(acc, width, 1)
            acc = acc[:, :lw2]
            if masked:
                k_chunk = (lax.broadcasted_iota(jnp.int32, (t1, lw2), 0) + r * t1) // CHUNK
                acc = jnp.where(k_chunk <= q_chunk, acc, -jnp.inf)
            sc_ref[pl.ds(row0, t1), :] = acc

    def pair_body(i, carry):
        score_chunk(2 * i, False)
        score_chunk(2 * i + 1, False)
        return carry

    lax.fori_loop(0, n_full // 2, pair_body, 0)

    @pl.when(n_full % 2 == 1)
    def _():
        score_chunk(n_full - 1, False)

    score_chunk(n_full, True)

    def count(preds):
        def body(c, accs):
            row0 = pl.multiple_of(c * ts, ts)
            blk = sc_ref[pl.ds(row0, ts), :]
            out = []
            for pred, acc in zip(preds, accs):
                m = jnp.where(pred(blk, row0), 1.0, 0.0)
                out.append(acc + m.reshape(ts // SUBLANES, SUBLANES, lw2).sum(axis=0))
            return tuple(out)
        init = tuple(jnp.zeros((SUBLANES, lw2), f32) for _ in preds)
        accs = lax.fori_loop(0, n_chunks, body, init)
        return [a.sum(axis=0, keepdims=True) for a in accs]

    def cut_counts(thr):
        c_ge, c_gt = count([lambda blk, row0: blk >= thr, lambda blk, row0: blk > thr])
        st_ref[0:1, :] = thr
        st_ref[1:2, :] = c_ge
        st_ref[2:3, :] = c_gt

    nc_max = cand_ref.shape[0]
    int_min = jnp.int32(INT_MIN)
    cand_ref[...] = jnp.zeros(cand_ref.shape, jnp.int32)

    def plane_body(c, carry):
        row0 = pl.multiple_of(c * ts, ts)
        for lt in range(lw2 // LANES):
            lanes = slice(lt * LANES, (lt + 1) * LANES)
            bits = pltpu.bitcast(sc_ref[pl.ds(row0, ts), lanes], jnp.int32)
            u = bits ^ ((bits >> 31) | int_min)
            w = [u[i * SUBLANES:(i + 1) * SUBLANES, :] for i in range(32)]
            j, m = 16, 0x0000FFFF
            while j:
                k = 0
                while k < 32:
                    t = (w[k] ^ (w[k + j] >> j)) & jnp.int32(m)
                    w[k] = w[k] ^ t
                    w[k + j] = w[k + j] ^ (t << j)
                    k = (k + j + 1) & ~j
                j >>= 1
                m = m ^ (m << j)
            gone = jnp.full((SUBLANES, LANES), -1, jnp.int32)
            for b in range(32):
                plane = w[31 - b]
                planes_ref[b, c, :, lanes] = plane
                gone = gone & (plane if (NEG_INF_WORD >> b) & 1 else ~plane)
            cand_ref[c, :, lanes] = ~gone
        return carry

    lax.fori_loop(0, n_chunks, plane_body, 0)

    def bit_count(b, flip):
        cnt = jnp.zeros((SUBLANES, lw2), jnp.int32)
        tot = jnp.zeros((SUBLANES, lw2), jnp.int32)
        for c in range(nc_max):
            cand = cand_ref[c]
            if flip is None:
                tot = tot + lax.population_count(cand)
            else:
                cand = cand & (planes_ref[b + 1, c] ^ flip)
                cand_ref[c] = cand
            cnt = cnt + lax.population_count(cand & planes_ref[b, c])
        red = lambda a: a.astype(f32).sum(axis=0, keepdims=True)
        return red(cnt), red(tot)

    def decide(c1, left, word, b):
        take = c1 >= left
        return (jnp.where(take, left, left - c1), word | jnp.where(take, lax.shift_left(jnp.int32(1), b), 0),
                jnp.where(take, 0, -1))

    c1, n_adm = bit_count(31, None)
    state = decide(c1, jnp.full((1, lw2), kf, f32), jnp.zeros((1, lw2), jnp.int32), 31)

    def radix_body(it, state):
        left, word, flip = state
        b = 30 - it
        c1, _ = bit_count(b, flip)
        return decide(c1, left, word, b)

    _, word, _ = lax.fori_loop(0, 31, radix_body, state)
    few = n_adm < kf
    cut_counts(jnp.where(few, F32_LOWEST, _key_to_f32(word ^ int_min)))

    good = jnp.logical_or(few, jnp.logical_and(st_ref[2:3, :] < kf, st_ref[1:2, :] >= kf))

    @pl.when(jnp.max(jnp.where(good, 0.0, 1.0)) > 0.0)
    def _():
        def bisect(it, key):
            cand = key + lax.shift_left(jnp.int32(1), 31 - it)
            thr = _key_to_f32(cand)
            cnt, = count([lambda blk, row0: blk >= thr])
            return jnp.where(cnt >= kf, cand, key)

        key = lax.fori_loop(0, 32, bisect, jnp.full((1, lw2), INT_MIN, jnp.int32))
        cut_counts(jnp.where(key == INT_MIN, F32_LOWEST, _key_to_f32(key)))

    thr, c_ge, c_gt = st_ref[0:1, :], st_ref[1:2, :], st_ref[2:3, :]
    ties = c_ge > kf
    need = kf - c_gt
    any_ties = jnp.max(jnp.where(ties, 1.0, 0.0))

    def write_bias(select):
        def body(c, carry):
            row0 = pl.multiple_of(c * ts, ts)
            blk = sc_ref[pl.ds(row0, ts), :]
            sc_ref[pl.ds(row0, ts), :] = jnp.where(select(blk, row0), 0.0, NEG_BIG)
            return carry
        lax.fori_loop(0, n_chunks, body, 0)

    @pl.when(any_ties == 0.0)
    def _():
        write_bias(lambda blk, row0: blk >= thr)

    @pl.when(any_ties > 0.0)
    def _():
        def rows(row0):
            return lax.broadcasted_iota(jnp.int32, (ts, lw2), 0) + row0

        def cut_bisect(it, cut):
            cand = cut + lax.shift_left(jnp.int32(1), idx_bits - 1 - it)
            cnt, = count([lambda blk, row0: jnp.logical_and(blk == thr, rows(row0) < cand)])
            return jnp.where(cnt < need, cand, cut)

        cut = lax.fori_loop(0, idx_bits, cut_bisect, jnp.zeros((1, lw2), jnp.int32))
        cut = jnp.where(ties, cut, jnp.int32(2 ** 30))
        write_bias(lambda blk, row0: jnp.logical_or(
            blk > thr, jnp.logical_and(blk == thr, rows(row0) <= cut)))

    m_ref[...] = jnp.full(m_ref.shape, NEG_BIG, f32)
    acc_ref[...] = jnp.zeros(acc_ref.shape, f32)

    def att_body(c, carry):
        row0 = pl.multiple_of(c * ta, ta)
        mx = []
        for hv in range(nvh):
            g = (hv * hp) // GROUP
            bias = sc_ref[pl.ds(row0, ta), :]
            if lw2 < lw:
                bias = jnp.concatenate([bias] * (lw // lw2), axis=1)
            kc = k_ref[0, g, pl.ds(row0, ta), :]
            s = lax.dot_general(kc, q_ref[hv], _NT, preferred_element_type=f32) + bias
            s_ref[hv] = s
            mx.append(s.reshape(ta // SUBLANES, SUBLANES, lw).max(axis=0).max(axis=0, keepdims=True))
        for hv in range(nvh):
            g = (hv * hp) // GROUP
            m_old = m_ref[hv]
            m_new = jnp.maximum(m_old, mx[hv])
            alpha = jnp.exp2(m_old - m_new)
            p = jnp.exp2(s_ref[hv] - m_new).astype(bf16)
            vt = vt_ref[0, g, :, pl.ds(row0, ta)]
            acc_ref[hv] = alpha * acc_ref[hv] + jnp.dot(vt, p, preferred_element_type=f32)
            m_ref[hv] = m_new
        return carry

    lax.fori_loop(0, n_chunks * (ts // ta), att_body, 0)
    for hv in range(nvh):
        o = acc_ref[hv]
        ot_ref[hv * HEAD_DIM:(hv + 1) * HEAD_DIM, :] = o[:HEAD_DIM] / o[HEAD_DIM:HEAD_DIM + 1]
    if hp == 1:
        o_ref[...] = ot_ref[...].T.astype(o_ref.dtype)
    else:
        o_ref[0] = ot_ref[...].astype(o_ref.dtype)


V_ROWS = HEAD_DIM + 16


def _attention(qi_hm, wi2d, q_hm, ki_all, k_hm, vt_aug, *, B, T, past, topk):
    tq = min(256, T)
    ts = 256
    ta = 256
    hp = ts // tq
    assert past % ts == 0 and (tq == ts or tq == T) and tq % CHUNK == 0 and hp in (1, 2, 4)
    nvh = N_HEADS // hp
    lw = hp * tq
    lw2 = max(tq, LANES)
    nq = T // tq
    nb = B * nq
    s_pad = ki_all.shape[1]
    assert s_pad == (past + T + ts - 1) // ts * ts
    N = B * T

    def pack(a):
        a = a.reshape(nvh, hp, nb, tq, HEAD_DIM)
        return jnp.transpose(a, (0, 2, 1, 3, 4)).reshape(nvh, nb * lw, HEAD_DIM)

    wi3 = jnp.transpose(wi2d.reshape(nb, tq, nvh, hp), (0, 2, 3, 1)).reshape(nb, nvh, lw)
    hm_spec = pl.BlockSpec((nvh, lw, HEAD_DIM), lambda b, j: (0, b * nq + j, 0))
    if hp == 1:
        out_spec = pl.BlockSpec((tq, ATTN_WIDTH), lambda b, j: (b * nq + j, 0))
        out_shape = jax.ShapeDtypeStruct((N, ATTN_WIDTH), bf16)
    else:
        out_spec = pl.BlockSpec((1, nvh * HEAD_DIM, lw), lambda b, j: (b * nq + j, 0, 0))
        out_shape = jax.ShapeDtypeStruct((nb, nvh * HEAD_DIM, lw), bf16)
    out = pl.pallas_call(
        functools.partial(_attn_kernel, tq=tq, hp=hp, ts=ts, t1=128, ta=ta, past=past, topk=topk,
                          idx_bits=max(1, math.ceil(math.log2(s_pad)))),
        grid=(B, nq),
        in_specs=[
            hm_spec,
            pl.BlockSpec((1, nvh, lw), lambda b, j: (b * nq + j, 0, 0)),
            hm_spec,
            pl.BlockSpec((1, s_pad, IDX_DIM), lambda b, j: (b, 0, 0)),
            pl.BlockSpec((1, N_KV_HEADS, s_pad, HEAD_DIM), lambda b, j: (b, 0, 0, 0)),
            pl.BlockSpec((1, N_KV_HEADS, V_ROWS, s_pad), lambda b, j: (b, 0, 0, 0)),
        ],
        out_specs=out_spec,
        out_shape=out_shape,
        scratch_shapes=[pltpu.VMEM((s_pad, lw2), f32), pltpu.VMEM((nvh, 1, lw), f32),
                        pltpu.VMEM((nvh, V_ROWS, lw), f32), pltpu.VMEM((nvh * HEAD_DIM, lw), f32),
                        pltpu.VMEM((nvh, ta, lw), f32), pltpu.VMEM((SUBLANES, lw2), f32),
                        pltpu.VMEM((32, s_pad // ts, SUBLANES, lw2), jnp.int32),
                        pltpu.VMEM((s_pad // ts, SUBLANES, lw2), jnp.int32)],
        compiler_params=pltpu.CompilerParams(dimension_semantics=("arbitrary", "arbitrary"),
                                             vmem_limit_bytes=VMEM_LIMIT),
        name="dsa_attention",
    )(pack(qi_hm), wi3, pack(q_hm), ki_all, k_hm, vt_aug)
    if hp > 1:
        out = out.reshape(nb, nvh, HEAD_DIM, hp, tq)
        out = jnp.transpose(out, (0, 4, 1, 3, 2)).reshape(N, ATTN_WIDTH)
    return out


FF_CHUNK = 256


def _post_kernel(x_ref, c_ref, a_ref, woc_ref, woa_ref, g1_ref, b1_ref, wg_ref, wu_ref, wd_ref,
                 g2_ref, b2_ref, o_ref, acc_ref):
    mix = (jnp.dot(c_ref[...], woc_ref[...], preferred_element_type=f32)
           + jnp.dot(a_ref[...], woa_ref[...], preferred_element_type=f32))
    x1 = _layer_norm(ALPHA * x_ref[...] + mix, g1_ref[...], b1_ref[...])
    x1b = x1.astype(bf16)
    for c in range(D_FF // FF_CHUNK):
        lo = c * FF_CHUNK
        gate = jnp.dot(x1b, wg_ref[:, lo:lo + FF_CHUNK], preferred_element_type=f32)
        up = jnp.dot(x1b, wu_ref[:, lo:lo + FF_CHUNK], preferred_element_type=f32)
        act = (gate * jax.nn.sigmoid(gate) * up).astype(bf16)
        part = jnp.dot(act, wd_ref[lo:lo + FF_CHUNK, :], preferred_element_type=f32)
        if c == 0:
            acc_ref[...] = part
        else:
            acc_ref[...] += part
    o_ref[...] = _layer_norm(ALPHA * x1 + acc_ref[...], g2_ref[...], b2_ref[...])


def _post(x2d, conv2d, attn2d, woc, woa, g1, b1, wg, wu, wd, g2, b2):
    N = x2d.shape[0]
    tm = min(512, N)
    row = lambda w: pl.BlockSpec((tm, w), lambda i: (i, 0))
    const = lambda a: pl.BlockSpec(a.shape, lambda i: (0, 0))
    vec = lambda a: a.reshape(1, D_MODEL)
    args = (x2d, conv2d, attn2d, woc, woa, vec(g1), vec(b1), wg, wu, wd, vec(g2), vec(b2))
    return pl.pallas_call(
        _post_kernel,
        grid=(N // tm,),
        in_specs=[row(D_MODEL), row(C_CONV), row(ATTN_WIDTH)] + [const(a) for a in args[3:]],
        out_specs=row(D_MODEL),
        out_shape=jax.ShapeDtypeStruct((N, D_MODEL), f32),
        scratch_shapes=[pltpu.VMEM((tm, D_MODEL), f32)],
        compiler_params=pltpu.CompilerParams(dimension_semantics=("arbitrary",), vmem_limit_bytes=VMEM_LIMIT),
        name="post",
    )(*args)


def _layer(x, conv_state, k_past, v_past, ki_past, wts):
    (w_pad, conv_w, conv_b, cln_g, cln_b, woc, woa, ln1_g, ln1_b, wg, wu, wd, ln2_g, ln2_b) = wts
    B, T, _ = x.shape
    past = k_past.shape[1]
    N = B * T
    x2d = x.reshape(N, D_MODEL)
    u, q_hm, k2d, v2d, qi_hm, kw = _inproj(x2d, w_pad, T, past)

    u3 = u.reshape(B, T, C_CONV)
    conv_out = _conv(u3, conv_state, conv_w, conv_b, cln_g, cln_b)
    new_conv = jnp.concatenate([conv_state.astype(f32), u3], axis=1)[:, -(CONV_WIDTH - 1):]

    k = k2d.reshape(B, T, N_KV_HEADS, HEAD_DIM)
    v = v2d.reshape(B, T, N_KV_HEADS, HEAD_DIM)
    ki = kw[:, :IDX_DIM].reshape(B, T, IDX_DIM)
    S = past + T
    s_pad = -(-S // 256) * 256
    pad_s = lambda a: jnp.pad(a, ((0, 0), (0, s_pad - S)) + ((0, 0),) * (a.ndim - 2))
    k_all = pad_s(jnp.concatenate([k_past.astype(bf16), k.astype(bf16)], axis=1))
    v_all = pad_s(jnp.concatenate([v_past.astype(bf16), v.astype(bf16)], axis=1))
    ki_all = pad_s(jnp.concatenate([ki_past.astype(bf16), ki.astype(bf16)], axis=1))
    k_hm = jnp.transpose(k_all, (0, 2, 1, 3))
    vt_hm = jnp.transpose(v_all, (0, 2, 3, 1))
    ones_rows = jnp.concatenate([jnp.ones((B, N_KV_HEADS, 1, s_pad), bf16),
                                 jnp.zeros((B, N_KV_HEADS, V_ROWS - HEAD_DIM - 1, s_pad), bf16)], axis=2)
    vt_aug = jnp.concatenate([vt_hm, ones_rows], axis=2)
    attn = _attention(qi_hm, kw[:, IDX_DIM:IDX_DIM + N_IDX_HEADS], q_hm, ki_all, k_hm, vt_aug,
                      B=B, T=T, past=past, topk=min(TOPK_MAX, S // 4))

    y = _post(x2d, conv_out.reshape(N, C_CONV), attn, woc, woa, ln1_g, ln1_b, wg, wu, wd, ln2_g, ln2_b)
    return y.reshape(B, T, D_MODEL), k, v, ki, new_conv


def kernel(x_prompt, x_sample, cache_k, cache_v, cache_k_idx, state_conv, w_in, conv_w, conv_b, conv_ln_g,
           conv_ln_b, w_o, ln1_g, ln1_b, w_gate_up, w_down, ln2_g, ln2_b):
    Bp = x_prompt.shape[0]
    dt = x_prompt.dtype
    empty_kv = jnp.zeros((Bp, 0, N_KV_HEADS, HEAD_DIM), dt)
    empty_ki = jnp.zeros((Bp, 0, IDX_DIM), dt)
    zero_conv = jnp.zeros((Bp, CONV_WIDTH - 1, C_CONV), dt)
    hp, hs = x_prompt, x_sample
    outs_p, outs_s = [], []
    for l in range(DEPTH):
        wts = (
            jnp.pad(w_in[l], ((0, 0), (0, IN_PAD - IN_DIM))).astype(bf16),
            conv_w[l], conv_b[l], conv_ln_g[l], conv_ln_b[l],
            w_o[l, :C_CONV].astype(bf16), w_o[l, C_CONV:].astype(bf16), ln1_g[l], ln1_b[l],
            w_gate_up[l, :, :D_FF].astype(bf16), w_gate_up[l, :, D_FF:].astype(bf16), w_down[l].astype(bf16),
            ln2_g[l], ln2_b[l],
        )
        hp, *op = _layer(hp, zero_conv, empty_kv, empty_kv, empty_ki, wts)
        hs, *os_ = _layer(hs, state_conv[l], cache_k[l], cache_v[l], cache_k_idx[l], wts)
        outs_p.append(op)
        outs_s.append(os_)
    stack = lambda outs, i: jnp.stack([o[i] for o in outs])
    return (hp, hs,
            stack(outs_p, 0), stack(outs_p, 1), stack(outs_p, 2), stack(outs_p, 3),
            stack(outs_s, 0), stack(outs_s, 1), stack(outs_s, 2), stack(outs_s, 3))
```

```python
import functools
import math

import jax
import jax.numpy as jnp
from jax import lax
from jax.experimental import pallas as pl
from jax.experimental.pallas import tpu as pltpu

D_MODEL = 1024
CHUNK = 64
C_CONV = D_MODEL // 2
CONV_WIDTH = 31
HEAD_DIM = 64
N_HEADS = (D_MODEL // 2) // HEAD_DIM
N_KV_HEADS = 2
GROUP = N_HEADS // N_KV_HEADS
ATTN_WIDTH = N_HEADS * HEAD_DIM
N_IDX_HEADS = 8
IDX_DIM = 64
TOPK_MAX = 256
ROPE_THETA = 10000.0
D_FF = -(-8 * D_MODEL // (3 * 256)) * 256
IN_DIM = 2 * C_CONV + (N_HEADS + 2 * N_KV_HEADS) * HEAD_DIM + N_IDX_HEADS * IDX_DIM + IDX_DIM + N_IDX_HEADS
DEPTH = 2
ALPHA = (2 * DEPTH) ** 0.25
ATTN_SCALE = HEAD_DIM ** -0.5
IDX_SCALE = IDX_DIM ** -0.5
IDX_HEAD_SCALE = N_IDX_HEADS ** -0.5
LN_EPS = 1e-5
Q_SCALE = ATTN_SCALE * math.log2(math.e)

LANES = 128
SUBLANES = 8
VMEM_LIMIT = 56 * 1024 * 1024
IN_PAD = -(-IN_DIM // LANES) * LANES
HALO = 32
NEG_BIG = -1e30
INT_MIN = -(2 ** 31)
F32_LOWEST = float(jnp.finfo(jnp.float32).min)
NEG_INF_WORD = 0x007FFFFF
V_ROWS = HEAD_DIM + 16

_O_A = 0
_O_G = _O_A + C_CONV
_O_Q = _O_G + C_CONV
_O_K = _O_Q + ATTN_WIDTH
_O_V = _O_K + N_KV_HEADS * HEAD_DIM
_O_QI = _O_V + N_KV_HEADS * HEAD_DIM
_O_KW = _O_QI + N_IDX_HEADS * IDX_DIM

_NT = (((1,), (1,)), ((), ()))

f32 = jnp.float32
bf16 = jnp.bfloat16


def _layer_norm(y, g, b):
    mu = jnp.mean(y, axis=-1, keepdims=True)
    d = y - mu
    var = jnp.mean(d * d, axis=-1, keepdims=True)
    return d * lax.rsqrt(var + LN_EPS) * g + b


def _rope128(x, cos, sa, sb):
    return x * cos + pltpu.roll(x, 96, 1) * sa + pltpu.roll(x, 32, 1) * sb


def _inproj_kernel(x_ref, w_ref, cos_ref, sa_ref, sb_ref, cosk_ref, sak_ref, sbk_ref,
                   u_ref, q_ref, k_ref, v_ref, qi_ref, kw_ref, kb_ref, kwb_ref, vt_ref):
    xb = x_ref[...].astype(bf16)
    tm = xb.shape[0]

    def mm(lo, width):
        return jnp.dot(xb, w_ref[:, lo:lo + width], preferred_element_type=f32)

    u_ref[...] = mm(_O_A, C_CONV) * jax.nn.sigmoid(mm(_O_G, C_CONV))

    cos, sa, sb = cos_ref[...], sa_ref[...], sb_ref[...]
    low = lax.broadcasted_iota(jnp.int32, (tm, LANES), 1) < HEAD_DIM

    def place(r, src, dst):
        if src != dst:
            r = pltpu.roll(r, HEAD_DIM, 1)
        return jnp.where(low if dst == 0 else jnp.logical_not(low), r, 0.0).astype(bf16)

    q = mm(_O_Q, ATTN_WIDTH)
    qi = mm(_O_QI, N_IDX_HEADS * IDX_DIM)
    for c in range(ATTN_WIDTH // LANES):
        r = _rope128(q[:, c * LANES:(c + 1) * LANES], cos, sa, sb) * Q_SCALE
        for half in range(2):
            h = 2 * c + half
            q_ref[h] = place(r, half, h // GROUP)
        r = _rope128(qi[:, c * LANES:(c + 1) * LANES], cos, sa, sb)
        for half in range(2):
            qi_ref[2 * c + half] = place(r, half, 0)
    k = _rope128(mm(_O_K, N_KV_HEADS * HEAD_DIM), cos, sa, sb)
    k_ref[...] = k
    kb_ref[...] = k.astype(bf16)
    v = mm(_O_V, N_KV_HEADS * HEAD_DIM)
    v_ref[...] = v
    vt = v.T.astype(bf16)
    extra = lax.broadcasted_iota(jnp.int32, (V_ROWS - HEAD_DIM, tm), 0) == 0
    for g in range(N_KV_HEADS):
        vt_ref[g, 0:HEAD_DIM, :] = vt[g * HEAD_DIM:(g + 1) * HEAD_DIM, :]
        vt_ref[g, HEAD_DIM:V_ROWS, :] = jnp.where(extra, 1.0, 0.0).astype(bf16)
    kw = _rope128(mm(_O_KW, LANES), cosk_ref[...], sak_ref[...], sbk_ref[...])
    kw_ref[...] = kw
    kwb_ref[...] = kw.astype(bf16)


def _rope_tables(T, past, rows):
    half = HEAD_DIM // 2
    inv = ROPE_THETA ** (-jnp.arange(half, dtype=f32) / half)
    pos = (jnp.arange(T, dtype=jnp.int32) + past).astype(f32)
    ang = pos[:, None] * inv[None, :]
    cos, sin = jnp.cos(ang), jnp.sin(ang)
    zero = jnp.zeros_like(sin)
    cos64 = jnp.concatenate([cos, cos], axis=1)
    sa64 = jnp.concatenate([-sin, zero], axis=1)
    sb64 = jnp.concatenate([zero, sin], axis=1)
    two = lambda a: jnp.concatenate([a, a], axis=1)
    wscale = jnp.full((T, N_IDX_HEADS), IDX_HEAD_SCALE * IDX_SCALE, f32)
    ones = jnp.ones((T, LANES - IDX_DIM - N_IDX_HEADS), f32)
    cosk = jnp.concatenate([cos64, wscale, ones], axis=1)
    zpad = jnp.zeros((T, LANES - IDX_DIM), f32)
    tabs = (two(cos64), two(sa64), two(sb64), cosk,
            jnp.concatenate([sa64, zpad], axis=1), jnp.concatenate([sb64, zpad], axis=1))
    reps = rows // T
    return tuple(jnp.tile(t, (reps, 1)) for t in tabs)


def _inproj(x2d, w_pad, T, past):
    N = x2d.shape[0]
    tm = min(512, N)
    rows = max(T, tm)
    tabs = _rope_tables(T, past, rows)
    nt = rows // tm
    row_spec = lambda w: pl.BlockSpec((tm, w), lambda i: (i, 0))
    tab_spec = pl.BlockSpec((tm, LANES), lambda i: (i % nt, 0))
    hm_spec = pl.BlockSpec((N_HEADS, tm, LANES), lambda i: (0, i, 0))
    return pl.pallas_call(
        _inproj_kernel,
        grid=(N // tm,),
        in_specs=[row_spec(D_MODEL), pl.BlockSpec((D_MODEL, IN_PAD), lambda i: (0, 0))] + [tab_spec] * 6,
        out_specs=[row_spec(C_CONV), hm_spec, row_spec(LANES), row_spec(LANES), hm_spec, row_spec(LANES),
                   row_spec(LANES), row_spec(LANES),
                   pl.BlockSpec((N_KV_HEADS, V_ROWS, tm), lambda i: (0, 0, i))],
        out_shape=[
            jax.ShapeDtypeStruct((N, C_CONV), f32),
            jax.ShapeDtypeStruct((N_HEADS, N, LANES), bf16),
            jax.ShapeDtypeStruct((N, N_KV_HEADS * HEAD_DIM), f32),
            jax.ShapeDtypeStruct((N, N_KV_HEADS * HEAD_DIM), f32),
            jax.ShapeDtypeStruct((N_IDX_HEADS, N, LANES), bf16),
            jax.ShapeDtypeStruct((N, LANES), f32),
            jax.ShapeDtypeStruct((N, LANES), bf16),
            jax.ShapeDtypeStruct((N, LANES), bf16),
            jax.ShapeDtypeStruct((N_KV_HEADS, V_ROWS, N), bf16),
        ],
        compiler_params=pltpu.CompilerParams(dimension_semantics=("arbitrary",), vmem_limit_bytes=VMEM_LIMIT),
        name="inproj",
    )(x2d, w_pad, *tabs)


def _conv_kernel(u_ref, halo_ref, st_ref, w_ref, b_ref, g_ref, bt_ref, o_ref, f_ref, *, tc, rb):
    i = pl.program_id(1)
    f_ref[0:HALO, :] = jnp.where(i == 0, st_ref[0], halo_ref[0])
    f_ref[HALO:HALO + tc, :] = u_ref[0]
    lead = HALO - (CONV_WIDTH - 1)
    for r in range(tc // rb):
        acc = jnp.zeros((rb, C_CONV), f32)
        for j in range(CONV_WIDTH):
            lo = r * rb + lead + j
            acc = acc + f_ref[lo:lo + rb, :] * w_ref[j:j + 1, :]
        y = _layer_norm(acc + b_ref[...], g_ref[...], bt_ref[...])
        o_ref[0, r * rb:(r + 1) * rb, :] = (y * jax.nn.sigmoid(y)).astype(o_ref.dtype)


def _conv(u3, state, conv_w, conv_b, ln_g, ln_b):
    B, T, _ = u3.shape
    tc = min(256, T)
    st = jnp.pad(state.astype(f32), ((0, 0), (HALO - (CONV_WIDTH - 1), 0), (0, 0)))
    wp = jnp.pad(conv_w, ((0, HALO - CONV_WIDTH), (0, 0)))
    vec = lambda a: a.reshape(1, C_CONV)
    hb = tc // HALO
    return pl.pallas_call(
        functools.partial(_conv_kernel, tc=tc, rb=32),
        grid=(B, T // tc),
        in_specs=[
            pl.BlockSpec((1, tc, C_CONV), lambda b, i: (b, i, 0)),
            pl.BlockSpec((1, HALO, C_CONV), lambda b, i: (b, jnp.maximum(i * hb - 1, 0), 0)),
            pl.BlockSpec((1, HALO, C_CONV), lambda b, i: (b, 0, 0)),
            pl.BlockSpec((HALO, C_CONV), lambda b, i: (0, 0)),
        ] + [pl.BlockSpec((1, C_CONV), lambda b, i: (0, 0))] * 3,
        out_specs=pl.BlockSpec((1, tc, C_CONV), lambda b, i: (b, i, 0)),
        out_shape=jax.ShapeDtypeStruct((B, T, C_CONV), bf16),
        scratch_shapes=[pltpu.VMEM((HALO + tc, C_CONV), f32)],
        compiler_params=pltpu.CompilerParams(dimension_semantics=("arbitrary", "arbitrary"),
                                             vmem_limit_bytes=VMEM_LIMIT),
        name="conv",
    )(u3, u3, st, wp, vec(conv_b), vec(ln_g), vec(ln_b))


def _key_to_f32(key):
    bits = key ^ ((key >> 31) & jnp.int32(0x7FFFFFFF))
    return pltpu.bitcast(bits, f32)


def _attn_kernel(qi_ref, wi_ref, q_ref, ki_ref, k_ref, vt_ref, o_ref, sc_ref, m_ref, acc_ref, ot_ref, s_ref,
                 mx_ref, st_ref, planes_ref, cand_ref,
                 *, tq, hp, ts, t1, ta, past, topk, idx_bits):
    nvh = N_HEADS // hp
    lw = hp * tq
    lw2 = sc_ref.shape[1]
    j = pl.program_id(1)
    n_full = (past + j * tq) // ts
    n_chunks = n_full + 1
    kf = float(topk)

    q_chunk = (lax.broadcasted_iota(jnp.int32, (t1, lw2), 1) % tq) // CHUNK

    def score_chunk(c, masked):
        for r in range(ts // t1):
            row0 = pl.multiple_of(c * ts + r * t1, t1)
            kic = ki_ref[pl.ds(row0, t1), :]
            acc = jnp.zeros((t1, lw), f32)
            for hv in range(nvh):
                d = lax.dot_general(kic, qi_ref[hv], _NT, preferred_element_type=f32)
                acc = acc + jnp.maximum(d, 0.0) * wi_ref[0, hv:hv + 1, :]
            width = lw
            while width > tq:
                width //= 2
                acc = acc + pltpu.roll(acc, width, 1)
            acc = acc[:, :lw2]
            if masked:
                k_chunk = (lax.broadcasted_iota(jnp.int32, (t1, lw2), 0) + r * t1) // CHUNK
                acc = jnp.where(k_chunk <= q_chunk, acc, -jnp.inf)
            sc_ref[pl.ds(row0, t1), :] = acc

    def pair_body(i, carry):
        score_chunk(2 * i, False)
        score_chunk(2 * i + 1, False)
        return carry

    lax.fori_loop(0, n_full // 2, pair_body, 0)

    @pl.when(n_full % 2 == 1)
    def _():
        score_chunk(n_full - 1, False)

    score_chunk(n_full, True)

    def count(preds):
        def body(c, accs):
            row0 = pl.multiple_of(c * ts, ts)
            blk = sc_ref[pl.ds(row0, ts), :]
            out = []
            for pred, acc in zip(preds, accs):
                m = jnp.where(pred(blk, row0), 1.0, 0.0)
                out.append(acc + m.reshape(ts // SUBLANES, SUBLANES, lw2).sum(axis=0))
            return tuple(out)
        init = tuple(jnp.zeros((SUBLANES, lw2), f32) for _ in preds)
        accs = lax.fori_loop(0, n_chunks, body, init)
        return [a.sum(axis=0, keepdims=True) for a in accs]

    def cut_counts(thr):
        c_ge, c_gt = count([lambda blk, row0: blk >= thr, lambda blk, row0: blk > thr])
        st_ref[0:1, :] = thr
        st_ref[1:2, :] = c_ge
        st_ref[2:3, :] = c_gt

    nc_max = cand_ref.shape[0]
    int_min = jnp.int32(INT_MIN)
    cand_ref[...] = jnp.zeros(cand_ref.shape, jnp.int32)

    def plane_body(c, carry):
        row0 = pl.multiple_of(c * ts, ts)
        for lt in range(lw2 // LANES):
            lanes = slice(lt * LANES, (lt + 1) * LANES)
            bits = pltpu.bitcast(sc_ref[pl.ds(row0, ts), lanes], jnp.int32)
            u = bits ^ ((bits >> 31) | int_min)
            w = [u[i * SUBLANES:(i + 1) * SUBLANES, :] for i in range(32)]
            j, m = 16, 0x0000FFFF
            while j:
                k = 0
                while k < 32:
                    t = (w[k] ^ (w[k + j] >> j)) & jnp.int32(m)
                    w[k] = w[k] ^ t
                    w[k + j] = w[k + j] ^ (t << j)
                    k = (k + j + 1) & ~j
                j >>= 1
                m = m ^ (m << j)
            gone = jnp.full((SUBLANES, LANES), -1, jnp.int32)
            for b in range(32):
                plane = w[31 - b]
                planes_ref[b, c, :, lanes] = plane
                gone = gone & (plane if (NEG_INF_WORD >> b) & 1 else ~plane)
            cand_ref[c, :, lanes] = ~gone
        return carry

    lax.fori_loop(0, n_chunks, plane_body, 0)

    def idle_body(c, carry):
        planes_ref[:, c] = jnp.zeros((32, SUBLANES, lw2), jnp.int32)
        return carry

    lax.fori_loop(n_chunks, nc_max, idle_body, 0)

    def bit_count(b, flip):
        cnt = jnp.zeros((SUBLANES, lw2), jnp.int32)
        tot = jnp.zeros((SUBLANES, lw2), jnp.int32)
        for c in range(nc_max):
            cand = cand_ref[c]
            if flip is None:
                tot = tot + lax.population_count(cand)
            else:
                cand = cand & (planes_ref[b + 1, c] ^ flip)
                cand_ref[c] = cand
            cnt = cnt + lax.population_count(cand & planes_ref[b, c])
        red = lambda a: a.astype(f32).sum(axis=0, keepdims=True)
        return red(cnt), red(tot)

    def decide(c1, left, word, b):
        take = c1 >= left
        return (jnp.where(take, left, left - c1), word | jnp.where(take, lax.shift_left(jnp.int32(1), b), 0),
                jnp.where(take, 0, -1))

    c1, n_adm = bit_count(31, None)
    state = decide(c1, jnp.full((1, lw2), kf, f32), jnp.zeros((1, lw2), jnp.int32), 31)

    def radix_body(it, state):
        left, word, flip = state
        b = 30 - it
        c1, _ = bit_count(b, flip)
        return decide(c1, left, word, b)

    _, word, _ = lax.fori_loop(0, 31, radix_body, state)
    few = n_adm < kf
    cut_counts(jnp.where(few, F32_LOWEST, _key_to_f32(word ^ int_min)))

    good = jnp.logical_or(few, jnp.logical_and(st_ref[2:3, :] < kf, st_ref[1:2, :] >= kf))

    @pl.when(jnp.max(jnp.where(good, 0.0, 1.0)) > 0.0)
    def _():
        def bisect(it, key):
            cand = key + lax.shift_left(jnp.int32(1), 31 - it)
            thr = _key_to_f32(cand)
            cnt, = count([lambda blk, row0: blk >= thr])
            return jnp.where(cnt >= kf, cand, key)

        key = lax.fori_loop(0, 32, bisect, jnp.full((1, lw2), INT_MIN, jnp.int32))
        cut_counts(jnp.where(key == INT_MIN, F32_LOWEST, _key_to_f32(key)))

    thr, c_ge, c_gt = st_ref[0:1, :], st_ref[1:2, :], st_ref[2:3, :]
    ties = c_ge > kf
    need = kf - c_gt
    any_ties = jnp.max(jnp.where(ties, 1.0, 0.0))

    def write_bias(select):
        def body(c, carry):
            row0 = pl.multiple_of(c * ts, ts)
            blk = sc_ref[pl.ds(row0, ts), :]
            sc_ref[pl.ds(row0, ts), :] = jnp.where(select(blk, row0), 0.0, NEG_BIG)
            return carry
        lax.fori_loop(0, n_chunks, body, 0)

    @pl.when(any_ties == 0.0)
    def _():
        write_bias(lambda blk, row0: blk >= thr)

    @pl.when(any_ties > 0.0)
    def _():
        def rows(row0):
            return lax.broadcasted_iota(jnp.int32, (ts, lw2), 0) + row0

        def cut_bisect(it, cut):
            cand = cut + lax.shift_left(jnp.int32(1), idx_bits - 1 - it)
            cnt, = count([lambda blk, row0: jnp.logical_and(blk == thr, rows(row0) < cand)])
            return jnp.where(cnt < need, cand, cut)

        cut = lax.fori_loop(0, idx_bits, cut_bisect, jnp.zeros((1, lw2), jnp.int32))
        cut = jnp.where(ties, cut, jnp.int32(2 ** 30))
        write_bias(lambda blk, row0: jnp.logical_or(
            blk > thr, jnp.logical_and(blk == thr, rows(row0) <= cut)))

    m_ref[...] = jnp.full(m_ref.shape, NEG_BIG, f32)
    acc_ref[...] = jnp.zeros(acc_ref.shape, f32)

    n_steps = n_chunks * (ts // ta)
    last = n_steps - 1

    def logits(step, buf):
        row0 = pl.multiple_of(step * ta, ta)
        kc = k_ref[pl.ds(row0, ta), :]
        for hv in range(nvh):
            bias = sc_ref[pl.ds(row0, ta), :]
            if lw2 < lw:
                bias = jnp.concatenate([bias] * (lw // lw2), axis=1)
            s = lax.dot_general(kc, q_ref[hv], _NT, preferred_element_type=f32) + bias
            s_ref[buf, hv] = s
            mx_ref[buf, hv] = s.reshape(ta // SUBLANES, SUBLANES, lw).max(axis=0).max(axis=0, keepdims=True)

    def absorb(step, buf):
        row0 = pl.multiple_of(step * ta, ta)
        for hv in range(nvh):
            g = (hv * hp) // GROUP
            m_old = m_ref[hv]
            m_new = jnp.maximum(m_old, mx_ref[buf, hv])
            alpha = jnp.exp2(m_old - m_new)
            p = jnp.exp2(s_ref[buf, hv] - m_new).astype(bf16)
            vt = vt_ref[g, :, pl.ds(row0, ta)]
            acc_ref[hv] = alpha * acc_ref[hv] + jnp.dot(vt, p, preferred_element_type=f32)
            m_ref[hv] = m_new

    logits(0, 0)

    def att_body(i, carry):
        logits(jnp.minimum(2 * i + 1, last), 1)
        absorb(2 * i, 0)
        logits(jnp.minimum(2 * i + 2, last), 0)
        absorb(2 * i + 1, 1)
        return carry

    lax.fori_loop(0, n_steps // 2, att_body, 0)

    @pl.when(n_steps % 2 == 1)
    def _():
        absorb(last, 0)

    for hv in range(nvh):
        o = acc_ref[hv]
        ot_ref[hv * HEAD_DIM:(hv + 1) * HEAD_DIM, :] = o[:HEAD_DIM] / o[HEAD_DIM:HEAD_DIM + 1]
    if hp == 1:
        o_ref[...] = ot_ref[...].T.astype(o_ref.dtype)
    else:
        o_ref[0] = ot_ref[...].astype(o_ref.dtype)


def _attention(qi_hm, wi2d, q_hm, ki_all, k_all, vt_all, *, B, T, past, topk):
    tq = min(256, T)
    ts = 256
    ta = 256
    hp = ts // tq
    assert past % ts == 0 and (tq == ts or tq == T) and tq % CHUNK == 0 and hp in (1, 2, 4)
    nvh = N_HEADS // hp
    lw = hp * tq
    lw2 = max(tq, LANES)
    nq = T // tq
    nb = B * nq
    s_pad = (past + T + ts - 1) // ts * ts
    assert ki_all.shape == (B * s_pad, LANES) and k_all.shape == (B * s_pad, LANES)
    assert vt_all.shape == (N_KV_HEADS, V_ROWS, B * s_pad)
    N = B * T

    def pack(a):
        if hp == 1:
            return a
        a = a.reshape(nvh, hp, nb, tq, LANES)
        return jnp.transpose(a, (0, 2, 1, 3, 4)).reshape(nvh, nb * lw, LANES)

    wi3 = jnp.transpose(wi2d.reshape(nb, tq, nvh, hp), (0, 2, 3, 1)).reshape(nb, nvh, lw)
    hm_spec = pl.BlockSpec((nvh, lw, LANES), lambda b, j: (0, b * nq + j, 0))
    if hp == 1:
        out_spec = pl.BlockSpec((tq, ATTN_WIDTH), lambda b, j: (b * nq + j, 0))
        out_shape = jax.ShapeDtypeStruct((N, ATTN_WIDTH), bf16)
    else:
        out_spec = pl.BlockSpec((1, nvh * HEAD_DIM, lw), lambda b, j: (b * nq + j, 0, 0))
        out_shape = jax.ShapeDtypeStruct((nb, nvh * HEAD_DIM, lw), bf16)
    out = pl.pallas_call(
        functools.partial(_attn_kernel, tq=tq, hp=hp, ts=ts, t1=128, ta=ta, past=past, topk=topk,
                          idx_bits=max(1, math.ceil(math.log2(s_pad)))),
        grid=(B, nq),
        in_specs=[
            hm_spec,
            pl.BlockSpec((1, nvh, lw), lambda b, j: (b * nq + j, 0, 0)),
            hm_spec,
            pl.BlockSpec((s_pad, LANES), lambda b, j: (b, 0)),
            pl.BlockSpec((s_pad, LANES), lambda b, j: (b, 0)),
            pl.BlockSpec((N_KV_HEADS, V_ROWS, s_pad), lambda b, j: (0, 0, b)),
        ],
        out_specs=out_spec,
        out_shape=out_shape,
        scratch_shapes=[pltpu.VMEM((s_pad, lw2), f32), pltpu.VMEM((nvh, 1, lw), f32),
                        pltpu.VMEM((nvh, V_ROWS, lw), f32), pltpu.VMEM((nvh * HEAD_DIM, lw), f32),
                        pltpu.VMEM((2, nvh, ta, lw), f32), pltpu.VMEM((2, nvh, 1, lw), f32),
                        pltpu.VMEM((SUBLANES, lw2), f32),
                        pltpu.VMEM((32, s_pad // ts, SUBLANES, lw2), jnp.int32),
                        pltpu.VMEM((s_pad // ts, SUBLANES, lw2), jnp.int32)],
        compiler_params=pltpu.CompilerParams(dimension_semantics=("arbitrary", "arbitrary"),
                                             vmem_limit_bytes=VMEM_LIMIT),
        name="dsa_attention",
    )(pack(qi_hm), wi3, pack(q_hm), ki_all, k_all, vt_all)
    if hp > 1:
        out = out.reshape(nb, nvh, HEAD_DIM, hp, tq)
        out = jnp.transpose(out, (0, 4, 1, 3, 2)).reshape(N, ATTN_WIDTH)
    return out


FF_CHUNK = 256


def _post_kernel(x_ref, c_ref, a_ref, woc_ref, woa_ref, g1_ref, b1_ref, wg_ref, wu_ref, wd_ref,
                 g2_ref, b2_ref, o_ref, acc_ref):
    mix = (jnp.dot(c_ref[...], woc_ref[...], preferred_element_type=f32)
           + jnp.dot(a_ref[...], woa_ref[...], preferred_element_type=f32))
    x1 = _layer_norm(ALPHA * x_ref[...] + mix, g1_ref[...], b1_ref[...])
    x1b = x1.astype(bf16)
    for c in range(D_FF // FF_CHUNK):
        lo = c * FF_CHUNK
        gate = jnp.dot(x1b, wg_ref[:, lo:lo + FF_CHUNK], preferred_element_type=f32)
        up = jnp.dot(x1b, wu_ref[:, lo:lo + FF_CHUNK], preferred_element_type=f32)
        act = (gate * jax.nn.sigmoid(gate) * up).astype(bf16)
        part = jnp.dot(act, wd_ref[lo:lo + FF_CHUNK, :], preferred_element_type=f32)
        if c == 0:
            acc_ref[...] = part
        else:
            acc_ref[...] += part
    o_ref[...] = _layer_norm(ALPHA * x1 + acc_ref[...], g2_ref[...], b2_ref[...])


def _post(x2d, conv2d, attn2d, woc, woa, g1, b1, wg, wu, wd, g2, b2):
    N = x2d.shape[0]
    tm = min(512, N)
    row = lambda w: pl.BlockSpec((tm, w), lambda i: (i, 0))
    const = lambda a: pl.BlockSpec(a.shape, lambda i: (0, 0))
    vec = lambda a: a.reshape(1, D_MODEL)
    args = (x2d, conv2d, attn2d, woc, woa, vec(g1), vec(b1), wg, wu, wd, vec(g2), vec(b2))
    return pl.pallas_call(
        _post_kernel,
        grid=(N // tm,),
        in_specs=[row(D_MODEL), row(C_CONV), row(ATTN_WIDTH)] + [const(a) for a in args[3:]],
        out_specs=row(D_MODEL),
        out_shape=jax.ShapeDtypeStruct((N, D_MODEL), f32),
        scratch_shapes=[pltpu.VMEM((tm, D_MODEL), f32)],
        compiler_params=pltpu.CompilerParams(dimension_semantics=("arbitrary",), vmem_limit_bytes=VMEM_LIMIT),
        name="post",
    )(*args)


def _layer(x, conv_state, k_past, v_past, ki_past, wts):
    (w_pad, conv_w, conv_b, cln_g, cln_b, woc, woa, ln1_g, ln1_b, wg, wu, wd, ln2_g, ln2_b) = wts
    B, T, _ = x.shape
    past = k_past.shape[1]
    N = B * T
    x2d = x.reshape(N, D_MODEL)
    u, q_hm, k2d, v2d, qi_hm, kw, kb, kwb, vt = _inproj(x2d, w_pad, T, past)

    u3 = u.reshape(B, T, C_CONV)
    conv_out = _conv(u3, conv_state, conv_w, conv_b, cln_g, cln_b)
    new_conv = jnp.concatenate([conv_state.astype(f32), u3], axis=1)[:, -(CONV_WIDTH - 1):]

    k = k2d.reshape(B, T, N_KV_HEADS, HEAD_DIM)
    v = v2d.reshape(B, T, N_KV_HEADS, HEAD_DIM)
    ki = kw[:, :IDX_DIM].reshape(B, T, IDX_DIM)
    S = past + T
    s_pad = -(-S // 256) * 256
    if past == 0 and s_pad == T:
        ki_all, k_all, vt_all = kwb, kb, vt
    else:
        def key_rows(old, new):
            a = jnp.concatenate([old.astype(bf16), new.reshape(B, T, LANES)], axis=1)
            return jnp.pad(a, ((0, 0), (0, s_pad - S), (0, 0))).reshape(B * s_pad, LANES)

        ki_all = key_rows(jnp.pad(ki_past, ((0, 0), (0, 0), (0, LANES - IDX_DIM))), kwb)
        k_all = key_rows(k_past.reshape(B, past, LANES), kb)
        extra = (jnp.arange(V_ROWS - HEAD_DIM) == 0).astype(bf16)
        vt_old = jnp.concatenate([
            jnp.transpose(v_past.astype(bf16), (2, 3, 0, 1)),
            jnp.broadcast_to(extra[None, :, None, None], (N_KV_HEADS, V_ROWS - HEAD_DIM, B, past))], axis=1)
        vt_all = jnp.concatenate([vt_old, vt.reshape(N_KV_HEADS, V_ROWS, B, T)], axis=3)
        vt_all = jnp.pad(vt_all, ((0, 0), (0, 0), (0, 0), (0, s_pad - S))).reshape(N_KV_HEADS, V_ROWS, B * s_pad)
    attn = _attention(qi_hm, kw[:, IDX_DIM:IDX_DIM + N_IDX_HEADS], q_hm, ki_all, k_all, vt_all,
                      B=B, T=T, past=past, topk=min(TOPK_MAX, S // 4))

    y = _post(x2d, conv_out.reshape(N, C_CONV), attn, woc, woa, ln1_g, ln1_b, wg, wu, wd, ln2_g, ln2_b)
    return y.reshape(B, T, D_MODEL), k, v, ki, new_conv


def kernel(x_prompt, x_sample, cache_k, cache_v, cache_k_idx, state_conv, w_in, conv_w, conv_b, conv_ln_g,
           conv_ln_b, w_o, ln1_g, ln1_b, w_gate_up, w_down, ln2_g, ln2_b):
    Bp = x_prompt.shape[0]
    dt = x_prompt.dtype
    empty_kv = jnp.zeros((Bp, 0, N_KV_HEADS, HEAD_DIM), dt)
    empty_ki = jnp.zeros((Bp, 0, IDX_DIM), dt)
    zero_conv = jnp.zeros((Bp, CONV_WIDTH - 1, C_CONV), dt)
    hp, hs = x_prompt, x_sample
    outs_p, outs_s = [], []
    for l in range(DEPTH):
        wts = (
            jnp.pad(w_in[l], ((0, 0), (0, IN_PAD - IN_DIM))).astype(bf16),
            conv_w[l], conv_b[l], conv_ln_g[l], conv_ln_b[l],
            w_o[l, :C_CONV].astype(bf16), w_o[l, C_CONV:].astype(bf16), ln1_g[l], ln1_b[l],
            w_gate_up[l, :, :D_FF].astype(bf16), w_gate_up[l, :, D_FF:].astype(bf16), w_down[l].astype(bf16),
            ln2_g[l], ln2_b[l],
        )
        hp, *op = _layer(hp, zero_conv, empty_kv, empty_kv, empty_ki, wts)
        hs, *os_ = _layer(hs, state_conv[l], cache_k[l], cache_v[l], cache_k_idx[l], wts)
        outs_p.append(op)
        outs_s.append(os_)
    stack = lambda outs, i: jnp.stack([o[i] for o in outs])
    return (hp, hs,
            stack(outs_p, 0), stack(outs_p, 1), stack(outs_p, 2), stack(outs_p, 3),
            stack(outs_s, 0), stack(outs_s, 1), stack(outs_s, 2), stack(outs_s, 3))
```

```python
import functools
import math

import jax
import jax.numpy as jnp
from jax import lax
from jax.experimental import pallas as pl
from jax.experimental.pallas import tpu as pltpu

D_MODEL = 1024
CHUNK = 64
C_CONV = D_MODEL // 2
CONV_WIDTH = 31
HEAD_DIM = 64
N_HEADS = (D_MODEL // 2) // HEAD_DIM
N_KV_HEADS = 2
GROUP = N_HEADS // N_KV_HEADS
ATTN_WIDTH = N_HEADS * HEAD_DIM
N_IDX_HEADS = 8
IDX_DIM = 64
TOPK_MAX = 256
ROPE_THETA = 10000.0
D_FF = -(-8 * D_MODEL // (3 * 256)) * 256
IN_DIM = 2 * C_CONV + (N_HEADS + 2 * N_KV_HEADS) * HEAD_DIM + N_IDX_HEADS * IDX_DIM + IDX_DIM + N_IDX_HEADS
DEPTH = 2
ALPHA = (2 * DEPTH) ** 0.25
ATTN_SCALE = HEAD_DIM ** -0.5
IDX_SCALE = IDX_DIM ** -0.5
IDX_HEAD_SCALE = N_IDX_HEADS ** -0.5
LN_EPS = 1e-5
Q_SCALE = ATTN_SCALE * math.log2(math.e)

LANES = 128
SUBLANES = 8
VMEM_LIMIT = 56 * 1024 * 1024
IN_PAD = -(-IN_DIM // LANES) * LANES
HALO = 32
NEG_BIG = -1e30
INT_MIN = -(2 ** 31)
F32_LOWEST = float(jnp.finfo(jnp.float32).min)
NEG_INF_WORD = 0x007FFFFF
V_ROWS = HEAD_DIM + 16

_O_A = 0
_O_G = _O_A + C_CONV
_O_Q = _O_G + C_CONV
_O_K = _O_Q + ATTN_WIDTH
_O_V = _O_K + N_KV_HEADS * HEAD_DIM
_O_QI = _O_V + N_KV_HEADS * HEAD_DIM
_O_KW = _O_QI + N_IDX_HEADS * IDX_DIM

_NT = (((1,), (1,)), ((), ()))

f32 = jnp.float32
bf16 = jnp.bfloat16


def _layer_norm(y, g, b):
    mu = jnp.mean(y, axis=-1, keepdims=True)
    d = y - mu
    var = jnp.mean(d * d, axis=-1, keepdims=True)
    return d * lax.rsqrt(var + LN_EPS) * g + b


def _rope128(x, cos, sa, sb):
    return x * cos + pltpu.roll(x, 96, 1) * sa + pltpu.roll(x, 32, 1) * sb


def _inproj_kernel(x_ref, w_ref, cos_ref, sa_ref, sb_ref, cosk_ref, sak_ref, sbk_ref,
                   u_ref, q_ref, k_ref, v_ref, qi_ref, kw_ref, kb_ref, kwb_ref, vt_ref):
    xb = x_ref[...].astype(bf16)
    tm = xb.shape[0]

    def mm(lo, width):
        return jnp.dot(xb, w_ref[:, lo:lo + width], preferred_element_type=f32)

    u_ref[...] = mm(_O_A, C_CONV) * jax.nn.sigmoid(mm(_O_G, C_CONV))

    cos, sa, sb = cos_ref[...], sa_ref[...], sb_ref[...]
    low = lax.broadcasted_iota(jnp.int32, (tm, LANES), 1) < HEAD_DIM

    def place(r, src, dst):
        if src != dst:
            r = pltpu.roll(r, HEAD_DIM, 1)
        return jnp.where(low if dst == 0 else jnp.logical_not(low), r, 0.0).astype(bf16)

    q = mm(_O_Q, ATTN_WIDTH)
    qi = mm(_O_QI, N_IDX_HEADS * IDX_DIM)
    for c in range(ATTN_WIDTH // LANES):
        r = _rope128(q[:, c * LANES:(c + 1) * LANES], cos, sa, sb) * Q_SCALE
        for half in range(2):
            h = 2 * c + half
            q_ref[h] = place(r, half, h // GROUP)
        r = _rope128(qi[:, c * LANES:(c + 1) * LANES], cos, sa, sb)
        for half in range(2):
            qi_ref[2 * c + half] = place(r, half, 0)
    k = _rope128(mm(_O_K, N_KV_HEADS * HEAD_DIM), cos, sa, sb)
    k_ref[...] = k
    kb_ref[...] = k.astype(bf16)
    v = mm(_O_V, N_KV_HEADS * HEAD_DIM)
    v_ref[...] = v
    vt = v.T.astype(bf16)
    extra = lax.broadcasted_iota(jnp.int32, (V_ROWS - HEAD_DIM, tm), 0) == 0
    for g in range(N_KV_HEADS):
        vt_ref[g, 0:HEAD_DIM, :] = vt[g * HEAD_DIM:(g + 1) * HEAD_DIM, :]
        vt_ref[g, HEAD_DIM:V_ROWS, :] = jnp.where(extra, 1.0, 0.0).astype(bf16)
    kw = _rope128(mm(_O_KW, LANES), cosk_ref[...], sak_ref[...], sbk_ref[...])
    kw_ref[...] = kw
    kwb_ref[...] = kw.astype(bf16)


def _rope_tables(T, past, rows):
    half = HEAD_DIM // 2
    inv = ROPE_THETA ** (-jnp.arange(half, dtype=f32) / half)
    pos = (jnp.arange(T, dtype=jnp.int32) + past).astype(f32)
    ang = pos[:, None] * inv[None, :]
    cos, sin = jnp.cos(ang), jnp.sin(ang)
    zero = jnp.zeros_like(sin)
    cos64 = jnp.concatenate([cos, cos], axis=1)
    sa64 = jnp.concatenate([-sin, zero], axis=1)
    sb64 = jnp.concatenate([zero, sin], axis=1)
    two = lambda a: jnp.concatenate([a, a], axis=1)
    wscale = jnp.full((T, N_IDX_HEADS), IDX_HEAD_SCALE * IDX_SCALE, f32)
    ones = jnp.ones((T, LANES - IDX_DIM - N_IDX_HEADS), f32)
    cosk = jnp.concatenate([cos64, wscale, ones], axis=1)
    zpad = jnp.zeros((T, LANES - IDX_DIM), f32)
    tabs = (two(cos64), two(sa64), two(sb64), cosk,
            jnp.concatenate([sa64, zpad], axis=1), jnp.concatenate([sb64, zpad], axis=1))
    reps = rows // T
    return tuple(jnp.tile(t, (reps, 1)) for t in tabs)


def _inproj(x2d, w_pad, T, past):
    N = x2d.shape[0]
    tm = min(512, N)
    rows = max(T, tm)
    tabs = _rope_tables(T, past, rows)
    nt = rows // tm
    row_spec = lambda w: pl.BlockSpec((tm, w), lambda i: (i, 0))
    tab_spec = pl.BlockSpec((tm, LANES), lambda i: (i % nt, 0))
    hm_spec = pl.BlockSpec((N_HEADS, tm, LANES), lambda i: (0, i, 0))
    return pl.pallas_call(
        _inproj_kernel,
        grid=(N // tm,),
        in_specs=[row_spec(D_MODEL), pl.BlockSpec((D_MODEL, IN_PAD), lambda i: (0, 0))] + [tab_spec] * 6,
        out_specs=[row_spec(C_CONV), hm_spec, row_spec(LANES), row_spec(LANES), hm_spec, row_spec(LANES),
                   row_spec(LANES), row_spec(LANES),
                   pl.BlockSpec((N_KV_HEADS, V_ROWS, tm), lambda i: (0, 0, i))],
        out_shape=[
            jax.ShapeDtypeStruct((N, C_CONV), f32),
            jax.ShapeDtypeStruct((N_HEADS, N, LANES), bf16),
            jax.ShapeDtypeStruct((N, N_KV_HEADS * HEAD_DIM), f32),
            jax.ShapeDtypeStruct((N, N_KV_HEADS * HEAD_DIM), f32),
            jax.ShapeDtypeStruct((N_IDX_HEADS, N, LANES), bf16),
            jax.ShapeDtypeStruct((N, LANES), f32),
            jax.ShapeDtypeStruct((N, LANES), bf16),
            jax.ShapeDtypeStruct((N, LANES), bf16),
            jax.ShapeDtypeStruct((N_KV_HEADS, V_ROWS, N), bf16),
        ],
        compiler_params=pltpu.CompilerParams(dimension_semantics=("arbitrary",), vmem_limit_bytes=VMEM_LIMIT),
        name="inproj",
    )(x2d, w_pad, *tabs)


def _conv_kernel(u_ref, halo_ref, st_ref, w_ref, b_ref, g_ref, bt_ref, o_ref, f_ref, sh_ref, *, tc, rb):
    i = pl.program_id(1)
    f_ref[0:HALO, :] = jnp.where(i == 0, st_ref[0], halo_ref[0])
    f_ref[HALO:HALO + tc, :] = u_ref[0]
    span = tc + HALO - SUBLANES
    for c in range(1, SUBLANES):
        sh_ref[c, 0:span, :] = f_ref[c:c + span, :]
    lead = HALO - (CONV_WIDTH - 1)
    for r in range(tc // rb):
        acc = jnp.zeros((rb, C_CONV), f32)
        for j in range(CONV_WIDTH):
            c = (lead + j) % SUBLANES
            lo = r * rb + lead + j - c
            rows = f_ref[lo:lo + rb, :] if c == 0 else sh_ref[c, lo:lo + rb, :]
            acc = acc + rows * jnp.tile(w_ref[j], (rb // SUBLANES, 1))
        y = _layer_norm(acc + b_ref[...], g_ref[...], bt_ref[...])
        o_ref[0, r * rb:(r + 1) * rb, :] = (y * jax.nn.sigmoid(y)).astype(o_ref.dtype)


def _conv(u3, state, conv_w, conv_b, ln_g, ln_b):
    B, T, _ = u3.shape
    tc = min(256, T)
    st = jnp.pad(state.astype(f32), ((0, 0), (HALO - (CONV_WIDTH - 1), 0), (0, 0)))
    wp = jnp.broadcast_to(conv_w[:, None, :], (CONV_WIDTH, SUBLANES, C_CONV))
    vec = lambda a: a.reshape(1, C_CONV)
    hb = tc // HALO
    return pl.pallas_call(
        functools.partial(_conv_kernel, tc=tc, rb=32),
        grid=(B, T // tc),
        in_specs=[
            pl.BlockSpec((1, tc, C_CONV), lambda b, i: (b, i, 0)),
            pl.BlockSpec((1, HALO, C_CONV), lambda b, i: (b, jnp.maximum(i * hb - 1, 0), 0)),
            pl.BlockSpec((1, HALO, C_CONV), lambda b, i: (b, 0, 0)),
            pl.BlockSpec((CONV_WIDTH, SUBLANES, C_CONV), lambda b, i: (0, 0, 0)),
        ] + [pl.BlockSpec((1, C_CONV), lambda b, i: (0, 0))] * 3,
        out_specs=pl.BlockSpec((1, tc, C_CONV), lambda b, i: (b, i, 0)),
        out_shape=jax.ShapeDtypeStruct((B, T, C_CONV), bf16),
        scratch_shapes=[pltpu.VMEM((HALO + tc, C_CONV), f32),
                        pltpu.VMEM((SUBLANES, HALO + tc - SUBLANES, C_CONV), f32)],
        compiler_params=pltpu.CompilerParams(dimension_semantics=("arbitrary", "arbitrary"),
                                             vmem_limit_bytes=VMEM_LIMIT),
        name="conv",
    )(u3, u3, st, wp, vec(conv_b), vec(ln_g), vec(ln_b))


def _key_to_f32(key):
    bits = key ^ ((key >> 31) & jnp.int32(0x7FFFFFFF))
    return pltpu.bitcast(bits, f32)


def _attn_kernel(qi_ref, wi_ref, q_ref, ki_ref, k_ref, vt_ref, o_ref, sc_ref, m_ref, acc_ref, ot_ref, s_ref,
                 mx_ref, st_ref, planes_ref, cand_ref,
                 *, tq, hp, ts, t1, ta, past, topk, idx_bits):
    nvh = N_HEADS // hp
    lw = hp * tq
    lw2 = sc_ref.shape[1]
    j = pl.program_id(1)
    n_full = (past + j * tq) // ts
    n_chunks = n_full + 1
    kf = float(topk)

    q_chunk = (lax.broadcasted_iota(jnp.int32, (t1, lw2), 1) % tq) // CHUNK

    nc_max = cand_ref.shape[0]
    int_min = jnp.int32(INT_MIN)
    cand_ref[...] = jnp.zeros(cand_ref.shape, jnp.int32)

    def write_planes(c):
        row0 = pl.multiple_of(c * ts, ts)
        for lt in range(lw2 // LANES):
            lanes = slice(lt * LANES, (lt + 1) * LANES)
            bits = pltpu.bitcast(sc_ref[pl.ds(row0, ts), lanes], jnp.int32)
            u = bits ^ ((bits >> 31) | int_min)
            w = [u[i * SUBLANES:(i + 1) * SUBLANES, :] for i in range(32)]
            j, m = 16, 0x0000FFFF
            while j:
                k = 0
                while k < 32:
                    t = (w[k] ^ (w[k + j] >> j)) & jnp.int32(m)
                    w[k] = w[k] ^ t
                    w[k + j] = w[k + j] ^ (t << j)
                    k = (k + j + 1) & ~j
                j >>= 1
                m = m ^ (m << j)
            gone = jnp.full((SUBLANES, LANES), -1, jnp.int32)
            for b in range(32):
                plane = w[31 - b]
                planes_ref[b, c, :, lanes] = plane
                gone = gone & (plane if (NEG_INF_WORD >> b) & 1 else ~plane)
            cand_ref[c, :, lanes] = ~gone

    def score_chunk(c, masked):
        for r in range(ts // t1):
            row0 = pl.multiple_of(c * ts + r * t1, t1)
            kic = ki_ref[pl.ds(row0, t1), :]
            acc = jnp.zeros((t1, lw), f32)
            for hv in range(nvh):
                d = lax.dot_general(kic, qi_ref[hv], _NT, preferred_element_type=f32)
                acc = acc + jnp.maximum(d, 0.0) * wi_ref[0, hv:hv + 1, :]
            width = lw
            while width > tq:
                width //= 2
                acc = acc + pltpu.roll(acc, width, 1)
            acc = acc[:, :lw2]
            if masked:
                k_chunk = (lax.broadcasted_iota(jnp.int32, (t1, lw2), 0) + r * t1) // CHUNK
                acc = jnp.where(k_chunk <= q_chunk, acc, -jnp.inf)
            sc_ref[pl.ds(row0, t1), :] = acc
        write_planes(c)

    def pair_body(i, carry):
        score_chunk(2 * i, False)
        score_chunk(2 * i + 1, False)
        return carry

    lax.fori_loop(0, n_full // 2, pair_body, 0)

    @pl.when(n_full % 2 == 1)
    def _():
        score_chunk(n_full - 1, False)

    score_chunk(n_full, True)

    def count(preds):
        def body(c, accs):
            row0 = pl.multiple_of(c * ts, ts)
            blk = sc_ref[pl.ds(row0, ts), :]
            out = []
            for pred, acc in zip(preds, accs):
                m = jnp.where(pred(blk, row0), 1.0, 0.0)
                out.append(acc + m.reshape(ts // SUBLANES, SUBLANES, lw2).sum(axis=0))
            return tuple(out)
        init = tuple(jnp.zeros((SUBLANES, lw2), f32) for _ in preds)
        accs = lax.fori_loop(0, n_chunks, body, init)
        return [a.sum(axis=0, keepdims=True) for a in accs]

    def cut_counts(thr):
        c_ge, c_gt = count([lambda blk, row0: blk >= thr, lambda blk, row0: blk > thr])
        st_ref[0:1, :] = thr
        st_ref[1:2, :] = c_ge
        st_ref[2:3, :] = c_gt

    def idle_body(c, carry):
        planes_ref[:, c] = jnp.zeros((32, SUBLANES, lw2), jnp.int32)
        return carry

    lax.fori_loop(n_chunks, nc_max, idle_body, 0)

    def bit_count(b, flip):
        cnt = jnp.zeros((SUBLANES, lw2), jnp.int32)
        tot = jnp.zeros((SUBLANES, lw2), jnp.int32)
        for c in range(nc_max):
            cand = cand_ref[c]
            if flip is None:
                tot = tot + lax.population_count(cand)
            else:
                cand = cand & (planes_ref[b + 1, c] ^ flip)
                cand_ref[c] = cand
            cnt = cnt + lax.population_count(cand & planes_ref[b, c])
        red = lambda a: a.astype(f32).sum(axis=0, keepdims=True)
        return red(cnt), red(tot)

    def decide(c1, left, word, b):
        take = c1 >= left
        return (jnp.where(take, left, left - c1), word | jnp.where(take, lax.shift_left(jnp.int32(1), b), 0),
                jnp.where(take, 0, -1))

    c1, n_adm = bit_count(31, None)
    state = decide(c1, jnp.full((1, lw2), kf, f32), jnp.zeros((1, lw2), jnp.int32), 31)

    def radix_body(it, state):
        left, word, flip = state
        b = 30 - it
        c1, _ = bit_count(b, flip)
        return decide(c1, left, word, b)

    _, word, _ = lax.fori_loop(0, 31, radix_body, state)
    few = n_adm < kf
    cut_counts(jnp.where(few, F32_LOWEST, _key_to_f32(word ^ int_min)))

    good = jnp.logical_or(few, jnp.logical_and(st_ref[2:3, :] < kf, st_ref[1:2, :] >= kf))

    @pl.when(jnp.max(jnp.where(good, 0.0, 1.0)) > 0.0)
    def _():
        def bisect(it, key):
            cand = key + lax.shift_left(jnp.int32(1), 31 - it)
            thr = _key_to_f32(cand)
            cnt, = count([lambda blk, row0: blk >= thr])
            return jnp.where(cnt >= kf, cand, key)

        key = lax.fori_loop(0, 32, bisect, jnp.full((1, lw2), INT_MIN, jnp.int32))
        cut_counts(jnp.where(key == INT_MIN, F32_LOWEST, _key_to_f32(key)))

    thr, c_ge, c_gt = st_ref[0:1, :], st_ref[1:2, :], st_ref[2:3, :]
    ties = c_ge > kf
    need = kf - c_gt
    any_ties = jnp.max(jnp.where(ties, 1.0, 0.0))

    def write_bias(select):
        def body(c, carry):
            row0 = pl.multiple_of(c * ts, ts)
            blk = sc_ref[pl.ds(row0, ts), :]
            sc_ref[pl.ds(row0, ts), :] = jnp.where(select(blk, row0), 0.0, NEG_BIG)
            return carry
        lax.fori_loop(0, n_chunks, body, 0)

    @pl.when(any_ties == 0.0)
    def _():
        write_bias(lambda blk, row0: blk >= thr)

    @pl.when(any_ties > 0.0)
    def _():
        def rows(row0):
            return lax.broadcasted_iota(jnp.int32, (ts, lw2), 0) + row0

        def cut_bisect(it, cut):
            cand = cut + lax.shift_left(jnp.int32(1), idx_bits - 1 - it)
            cnt, = count([lambda blk, row0: jnp.logical_and(blk == thr, rows(row0) < cand)])
            return jnp.where(cnt < need, cand, cut)

        cut = lax.fori_loop(0, idx_bits, cut_bisect, jnp.zeros((1, lw2), jnp.int32))
        cut = jnp.where(ties, cut, jnp.int32(2 ** 30))
        write_bias(lambda blk, row0: jnp.logical_or(
            blk > thr, jnp.logical_and(blk == thr, rows(row0) <= cut)))

    m_ref[...] = jnp.full(m_ref.shape, NEG_BIG, f32)
    acc_ref[...] = jnp.zeros(acc_ref.shape, f32)

    n_steps = n_chunks * (ts // ta)
    last = n_steps - 1

    def logits(step, buf):
        row0 = pl.multiple_of(step * ta, ta)
        kc = k_ref[pl.ds(row0, ta), :]
        for hv in range(nvh):
            bias = sc_ref[pl.ds(row0, ta), :]
            if lw2 < lw:
                bias = jnp.concatenate([bias] * (lw // lw2), axis=1)
            s = lax.dot_general(kc, q_ref[hv], _NT, preferred_element_type=f32) + bias
            s_ref[buf, hv] = s
            mx_ref[buf, hv] = s.reshape(ta // SUBLANES, SUBLANES, lw).max(axis=0).max(axis=0, keepdims=True)

    def absorb(step, buf):
        row0 = pl.multiple_of(step * ta, ta)
        for hv in range(nvh):
            g = (hv * hp) // GROUP
            m_old = m_ref[hv]
            m_new = jnp.maximum(m_old, mx_ref[buf, hv])
            alpha = jnp.exp2(m_old - m_new)
            p = jnp.exp2(s_ref[buf, hv] - m_new).astype(bf16)
            vt = vt_ref[g, :, pl.ds(row0, ta)]
            acc_ref[hv] = alpha * acc_ref[hv] + jnp.dot(vt, p, preferred_element_type=f32)
            m_ref[hv] = m_new

    logits(0, 0)

    def att_body(i, carry):
        logits(jnp.minimum(2 * i + 1, last), 1)
        absorb(2 * i, 0)
        logits(jnp.minimum(2 * i + 2, last), 0)
        absorb(2 * i + 1, 1)
        return carry

    lax.fori_loop(0, n_steps // 2, att_body, 0)

    @pl.when(n_steps % 2 == 1)
    def _():
        absorb(last, 0)

    for hv in range(nvh):
        o = acc_ref[hv]
        ot_ref[hv * HEAD_DIM:(hv + 1) * HEAD_DIM, :] = o[:HEAD_DIM] / o[HEAD_DIM:HEAD_DIM + 1]
    if hp == 1:
        o_ref[...] = ot_ref[...].T.astype(o_ref.dtype)
    else:
        o_ref[0] = ot_ref[...].astype(o_ref.dtype)


def _attention(qi_hm, wi2d, q_hm, ki_all, k_all, vt_all, *, B, T, past, topk):
    tq = min(256, T)
    ts = 256
    ta = 256
    hp = ts // tq
    assert past % ts == 0 and (tq == ts or tq == T) and tq % CHUNK == 0 and hp in (1, 2, 4)
    nvh = N_HEADS // hp
    lw = hp * tq
    lw2 = max(tq, LANES)
    nq = T // tq
    nb = B * nq
    s_pad = (past + T + ts - 1) // ts * ts
    assert ki_all.shape == (B * s_pad, LANES) and k_all.shape == (B * s_pad, LANES)
    assert vt_all.shape == (N_KV_HEADS, V_ROWS, B * s_pad)
    N = B * T

    def pack(a):
        if hp == 1:
            return a
        a = a.reshape(nvh, hp, nb, tq, LANES)
        return jnp.transpose(a, (0, 2, 1, 3, 4)).reshape(nvh, nb * lw, LANES)

    wi3 = jnp.transpose(wi2d.reshape(nb, tq, nvh, hp), (0, 2, 3, 1)).reshape(nb, nvh, lw)
    hm_spec = pl.BlockSpec((nvh, lw, LANES), lambda b, j: (0, b * nq + j, 0))
    if hp == 1:
        out_spec = pl.BlockSpec((tq, ATTN_WIDTH), lambda b, j: (b * nq + j, 0))
        out_shape = jax.ShapeDtypeStruct((N, ATTN_WIDTH), bf16)
    else:
        out_spec = pl.BlockSpec((1, nvh * HEAD_DIM, lw), lambda b, j: (b * nq + j, 0, 0))
        out_shape = jax.ShapeDtypeStruct((nb, nvh * HEAD_DIM, lw), bf16)
    out = pl.pallas_call(
        functools.partial(_attn_kernel, tq=tq, hp=hp, ts=ts, t1=128, ta=ta, past=past, topk=topk,
                          idx_bits=max(1, math.ceil(math.log2(s_pad)))),
        grid=(B, nq),
        in_specs=[
            hm_spec,
            pl.BlockSpec((1, nvh, lw), lambda b, j: (b * nq + j, 0, 0)),
            hm_spec,
            pl.BlockSpec((s_pad, LANES), lambda b, j: (b, 0)),
            pl.BlockSpec((s_pad, LANES), lambda b, j: (b, 0)),
            pl.BlockSpec((N_KV_HEADS, V_ROWS, s_pad), lambda b, j: (0, 0, b)),
        ],
        out_specs=out_spec,
        out_shape=out_shape,
        scratch_shapes=[pltpu.VMEM((s_pad, lw2), f32), pltpu.VMEM((nvh, 1, lw), f32),
                        pltpu.VMEM((nvh, V_ROWS, lw), f32), pltpu.VMEM((nvh * HEAD_DIM, lw), f32),
                        pltpu.VMEM((2, nvh, ta, lw), f32), pltpu.VMEM((2, nvh, 1, lw), f32),
                        pltpu.VMEM((SUBLANES, lw2), f32),
                        pltpu.VMEM((32, s_pad // ts, SUBLANES, lw2), jnp.int32),
                        pltpu.VMEM((s_pad // ts, SUBLANES, lw2), jnp.int32)],
        compiler_params=pltpu.CompilerParams(dimension_semantics=("arbitrary", "arbitrary"),
                                             vmem_limit_bytes=VMEM_LIMIT),
        name="dsa_attention",
    )(pack(qi_hm), wi3, pack(q_hm), ki_all, k_all, vt_all)
    if hp > 1:
        out = out.reshape(nb, nvh, HEAD_DIM, hp, tq)
        out = jnp.transpose(out, (0, 4, 1, 3, 2)).reshape(N, ATTN_WIDTH)
    return out


FF_CHUNK = 256


def _post_kernel(x_ref, c_ref, a_ref, woc_ref, woa_ref, g1_ref, b1_ref, wg_ref, wu_ref, wd_ref,
                 g2_ref, b2_ref, o_ref, acc_ref):
    mix = (jnp.dot(c_ref[...], woc_ref[...], preferred_element_type=f32)
           + jnp.dot(a_ref[...], woa_ref[...], preferred_element_type=f32))
    x1 = _layer_norm(ALPHA * x_ref[...] + mix, g1_ref[...], b1_ref[...])
    x1b = x1.astype(bf16)
    for c in range(D_FF // FF_CHUNK):
        lo = c * FF_CHUNK
        gate = jnp.dot(x1b, wg_ref[:, lo:lo + FF_CHUNK], preferred_element_type=f32)
        up = jnp.dot(x1b, wu_ref[:, lo:lo + FF_CHUNK], preferred_element_type=f32)
        act = (gate * jax.nn.sigmoid(gate) * up).astype(bf16)
        part = jnp.dot(act, wd_ref[lo:lo + FF_CHUNK, :], preferred_element_type=f32)
        if c == 0:
            acc_ref[...] = part
        else:
            acc_ref[...] += part
    o_ref[...] = _layer_norm(ALPHA * x1 + acc_ref[...], g2_ref[...], b2_ref[...])


def _post(x2d, conv2d, attn2d, woc, woa, g1, b1, wg, wu, wd, g2, b2):
    N = x2d.shape[0]
    tm = min(512, N)
    row = lambda w: pl.BlockSpec((tm, w), lambda i: (i, 0))
    const = lambda a: pl.BlockSpec(a.shape, lambda i: (0, 0))
    vec = lambda a: a.reshape(1, D_MODEL)
    args = (x2d, conv2d, attn2d, woc, woa, vec(g1), vec(b1), wg, wu, wd, vec(g2), vec(b2))
    return pl.pallas_call(
        _post_kernel,
        grid=(N // tm,),
        in_specs=[row(D_MODEL), row(C_CONV), row(ATTN_WIDTH)] + [const(a) for a in args[3:]],
        out_specs=row(D_MODEL),
        out_shape=jax.ShapeDtypeStruct((N, D_MODEL), f32),
        scratch_shapes=[pltpu.VMEM((tm, D_MODEL), f32)],
        compiler_params=pltpu.CompilerParams(dimension_semantics=("arbitrary",), vmem_limit_bytes=VMEM_LIMIT),
        name="post",
    )(*args)


def _layer(x, conv_state, k_past, v_past, ki_past, wts):
    (w_pad, conv_w, conv_b, cln_g, cln_b, woc, woa, ln1_g, ln1_b, wg, wu, wd, ln2_g, ln2_b) = wts
    B, T, _ = x.shape
    past = k_past.shape[1]
    N = B * T
    x2d = x.reshape(N, D_MODEL)
    u, q_hm, k2d, v2d, qi_hm, kw, kb, kwb, vt = _inproj(x2d, w_pad, T, past)

    u3 = u.reshape(B, T, C_CONV)
    conv_out = _conv(u3, conv_state, conv_w, conv_b, cln_g, cln_b)
    new_conv = jnp.concatenate([conv_state.astype(f32), u3], axis=1)[:, -(CONV_WIDTH - 1):]

    k = k2d.reshape(B, T, N_KV_HEADS, HEAD_DIM)
    v = v2d.reshape(B, T, N_KV_HEADS, HEAD_DIM)
    ki = kw[:, :IDX_DIM].reshape(B, T, IDX_DIM)
    S = past + T
    s_pad = -(-S // 256) * 256
    if past == 0 and s_pad == T:
        ki_all, k_all, vt_all = kwb, kb, vt
    else:
        def key_rows(old, new):
            a = jnp.concatenate([old, new.reshape(B, T, LANES)], axis=1)
            return jnp.pad(a, ((0, 0), (0, s_pad - S), (0, 0))).reshape(B * s_pad, LANES)

        ki_all = key_rows(jnp.pad(ki_past.astype(bf16), ((0, 0), (0, 0), (0, LANES - IDX_DIM))), kwb)
        k_all = key_rows(k_past.astype(bf16).reshape(B, past, LANES), kb)
        extra = (jnp.arange(V_ROWS - HEAD_DIM) == 0).astype(bf16)
        vt_old = jnp.concatenate([
            jnp.transpose(v_past.astype(bf16), (2, 3, 0, 1)),
            jnp.broadcast_to(extra[None, :, None, None], (N_KV_HEADS, V_ROWS - HEAD_DIM, B, past))], axis=1)
        vt_all = jnp.concatenate([vt_old, vt.reshape(N_KV_HEADS, V_ROWS, B, T)], axis=3)
        vt_all = jnp.pad(vt_all, ((0, 0), (0, 0), (0, 0), (0, s_pad - S))).reshape(N_KV_HEADS, V_ROWS, B * s_pad)
    attn = _attention(qi_hm, kw[:, IDX_DIM:IDX_DIM + N_IDX_HEADS], q_hm, ki_all, k_all, vt_all,
                      B=B, T=T, past=past, topk=min(TOPK_MAX, S // 4))

    y = _post(x2d, conv_out.reshape(N, C_CONV), attn, woc, woa, ln1_g, ln1_b, wg, wu, wd, ln2_g, ln2_b)
    return y.reshape(B, T, D_MODEL), k, v, ki, new_conv


def kernel(x_prompt, x_sample, cache_k, cache_v, cache_k_idx, state_conv, w_in, conv_w, conv_b, conv_ln_g,
           conv_ln_b, w_o, ln1_g, ln1_b, w_gate_up, w_down, ln2_g, ln2_b):
    Bp = x_prompt.shape[0]
    dt = x_prompt.dtype
    empty_kv = jnp.zeros((Bp, 0, N_KV_HEADS, HEAD_DIM), dt)
    empty_ki = jnp.zeros((Bp, 0, IDX_DIM), dt)
    zero_conv = jnp.zeros((Bp, CONV_WIDTH - 1, C_CONV), dt)
    hp, hs = x_prompt, x_sample
    outs_p, outs_s = [], []
    for l in range(DEPTH):
        wts = (
            jnp.pad(w_in[l], ((0, 0), (0, IN_PAD - IN_DIM))).astype(bf16),
            conv_w[l], conv_b[l], conv_ln_g[l], conv_ln_b[l],
            w_o[l, :C_CONV].astype(bf16), w_o[l, C_CONV:].astype(bf16), ln1_g[l], ln1_b[l],
            w_gate_up[l, :, :D_FF].astype(bf16), w_gate_up[l, :, D_FF:].astype(bf16), w_down[l].astype(bf16),
            ln2_g[l], ln2_b[l],
        )
        hp, *op = _layer(hp, zero_conv, empty_kv, empty_kv, empty_ki, wts)
        hs, *os_ = _layer(hs, state_conv[l], cache_k[l], cache_v[l], cache_k_idx[l], wts)
        outs_p.append(op)
        outs_s.append(os_)
    stack = lambda outs, i: jnp.stack([o[i] for o in outs])
    return (hp, hs,
            stack(outs_p, 0), stack(outs_p, 1), stack(outs_p, 2), stack(outs_p, 3),
            stack(outs_s, 0), stack(outs_s, 1), stack(outs_s, 2), stack(outs_s, 3))
```

```python
import functools
import math

import jax
import jax.numpy as jnp
from jax import lax
from jax.experimental import pallas as pl
from jax.experimental.pallas import tpu as pltpu

D_MODEL = 1024
CHUNK = 64
C_CONV = D_MODEL // 2
CONV_WIDTH = 31
HEAD_DIM = 64
N_HEADS = (D_MODEL // 2) // HEAD_DIM
N_KV_HEADS = 2
GROUP = N_HEADS // N_KV_HEADS
ATTN_WIDTH = N_HEADS * HEAD_DIM
N_IDX_HEADS = 8
IDX_DIM = 64
TOPK_MAX = 256
ROPE_THETA = 10000.0
D_FF = -(-8 * D_MODEL // (3 * 256)) * 256
IN_DIM = 2 * C_CONV + (N_HEADS + 2 * N_KV_HEADS) * HEAD_DIM + N_IDX_HEADS * IDX_DIM + IDX_DIM + N_IDX_HEADS
DEPTH = 2
ALPHA = (2 * DEPTH) ** 0.25
ATTN_SCALE = HEAD_DIM ** -0.5
IDX_SCALE = IDX_DIM ** -0.5
IDX_HEAD_SCALE = N_IDX_HEADS ** -0.5
LN_EPS = 1e-5
Q_SCALE = ATTN_SCALE * math.log2(math.e)

LANES = 128
SUBLANES = 8
VMEM_LIMIT = 56 * 1024 * 1024
IN_PAD = -(-IN_DIM // LANES) * LANES
HALO = 32
NEG_BIG = -1e30
INT_MIN = -(2 ** 31)
F32_LOWEST = float(jnp.finfo(jnp.float32).min)
NEG_INF_WORD = 0x007FFFFF
V_ROWS = HEAD_DIM + 16

_O_A = 0
_O_G = _O_A + C_CONV
_O_Q = _O_G + C_CONV
_O_K = _O_Q + ATTN_WIDTH
_O_V = _O_K + N_KV_HEADS * HEAD_DIM
_O_QI = _O_V + N_KV_HEADS * HEAD_DIM
_O_KW = _O_QI + N_IDX_HEADS * IDX_DIM

_NT = (((1,), (1,)), ((), ()))

f32 = jnp.float32
bf16 = jnp.bfloat16


def _layer_norm(y, g, b):
    mu = jnp.mean(y, axis=-1, keepdims=True)
    d = y - mu
    var = jnp.mean(d * d, axis=-1, keepdims=True)
    return d * lax.rsqrt(var + LN_EPS) * g + b


def _rope128(x, cos, sa, sb):
    return x * cos + pltpu.roll(x, 96, 1) * sa + pltpu.roll(x, 32, 1) * sb


def _inproj_kernel(x_ref, w_ref, cos_ref, sa_ref, sb_ref, cosk_ref, sak_ref, sbk_ref,
                   u_ref, q_ref, k_ref, v_ref, qi_ref, kw_ref, kb_ref, kwb_ref, vt_ref):
    xb = x_ref[...].astype(bf16)
    tm = xb.shape[0]

    def mm(lo, width):
        return jnp.dot(xb, w_ref[:, lo:lo + width], preferred_element_type=f32)

    u_ref[...] = mm(_O_A, C_CONV) * jax.nn.sigmoid(mm(_O_G, C_CONV))

    cos, sa, sb = cos_ref[...], sa_ref[...], sb_ref[...]
    low = lax.broadcasted_iota(jnp.int32, (tm, LANES), 1) < HEAD_DIM

    def place(r, src, dst):
        if src != dst:
            r = pltpu.roll(r, HEAD_DIM, 1)
        return jnp.where(low if dst == 0 else jnp.logical_not(low), r, 0.0).astype(bf16)

    q = mm(_O_Q, ATTN_WIDTH)
    qi = mm(_O_QI, N_IDX_HEADS * IDX_DIM)
    for c in range(ATTN_WIDTH // LANES):
        r = _rope128(q[:, c * LANES:(c + 1) * LANES], cos, sa, sb) * Q_SCALE
        for half in range(2):
            h = 2 * c + half
            q_ref[h] = place(r, half, h // GROUP)
        r = _rope128(qi[:, c * LANES:(c + 1) * LANES], cos, sa, sb)
        for half in range(2):
            qi_ref[2 * c + half] = place(r, half, 0)
    k = _rope128(mm(_O_K, N_KV_HEADS * HEAD_DIM), cos, sa, sb)
    k_ref[...] = k
    kb_ref[...] = k.astype(bf16)
    v = mm(_O_V, N_KV_HEADS * HEAD_DIM)
    v_ref[...] = v
    vt = v.T.astype(bf16)
    extra = lax.broadcasted_iota(jnp.int32, (V_ROWS - HEAD_DIM, tm), 0) == 0
    for g in range(N_KV_HEADS):
        vt_ref[g, 0:HEAD_DIM, :] = vt[g * HEAD_DIM:(g + 1) * HEAD_DIM, :]
        vt_ref[g, HEAD_DIM:V_ROWS, :] = jnp.where(extra, 1.0, 0.0).astype(bf16)
    kw = _rope128(mm(_O_KW, LANES), cosk_ref[...], sak_ref[...], sbk_ref[...])
    kw_ref[...] = kw
    kwb_ref[...] = kw.astype(bf16)


def _rope_tables(T, past, rows):
    half = HEAD_DIM // 2
    inv = ROPE_THETA ** (-jnp.arange(half, dtype=f32) / half)
    pos = (jnp.arange(T, dtype=jnp.int32) + past).astype(f32)
    ang = pos[:, None] * inv[None, :]
    cos, sin = jnp.cos(ang), jnp.sin(ang)
    zero = jnp.zeros_like(sin)
    cos64 = jnp.concatenate([cos, cos], axis=1)
    sa64 = jnp.concatenate([-sin, zero], axis=1)
    sb64 = jnp.concatenate([zero, sin], axis=1)
    two = lambda a: jnp.concatenate([a, a], axis=1)
    wscale = jnp.full((T, N_IDX_HEADS), IDX_HEAD_SCALE * IDX_SCALE, f32)
    ones = jnp.ones((T, LANES - IDX_DIM - N_IDX_HEADS), f32)
    cosk = jnp.concatenate([cos64, wscale, ones], axis=1)
    zpad = jnp.zeros((T, LANES - IDX_DIM), f32)
    tabs = (two(cos64), two(sa64), two(sb64), cosk,
            jnp.concatenate([sa64, zpad], axis=1), jnp.concatenate([sb64, zpad], axis=1))
    reps = rows // T
    return tuple(jnp.tile(t, (reps, 1)) for t in tabs)


def _inproj(x2d, w_pad, T, past):
    N = x2d.shape[0]
    tm = min(512, N)
    rows = max(T, tm)
    tabs = _rope_tables(T, past, rows)
    nt = rows // tm
    row_spec = lambda w: pl.BlockSpec((tm, w), lambda i: (i, 0))
    tab_spec = pl.BlockSpec((tm, LANES), lambda i: (i % nt, 0))
    hm_spec = pl.BlockSpec((N_HEADS, tm, LANES), lambda i: (0, i, 0))
    return pl.pallas_call(
        _inproj_kernel,
        grid=(N // tm,),
        in_specs=[row_spec(D_MODEL), pl.BlockSpec((D_MODEL, IN_PAD), lambda i: (0, 0))] + [tab_spec] * 6,
        out_specs=[row_spec(C_CONV), hm_spec, row_spec(LANES), row_spec(LANES), hm_spec, row_spec(LANES),
                   row_spec(LANES), row_spec(LANES),
                   pl.BlockSpec((N_KV_HEADS, V_ROWS, tm), lambda i: (0, 0, i))],
        out_shape=[
            jax.ShapeDtypeStruct((N, C_CONV), f32),
            jax.ShapeDtypeStruct((N_HEADS, N, LANES), bf16),
            jax.ShapeDtypeStruct((N, N_KV_HEADS * HEAD_DIM), f32),
            jax.ShapeDtypeStruct((N, N_KV_HEADS * HEAD_DIM), f32),
            jax.ShapeDtypeStruct((N_IDX_HEADS, N, LANES), bf16),
            jax.ShapeDtypeStruct((N, LANES), f32),
            jax.ShapeDtypeStruct((N, LANES), bf16),
            jax.ShapeDtypeStruct((N, LANES), bf16),
            jax.ShapeDtypeStruct((N_KV_HEADS, V_ROWS, N), bf16),
        ],
        compiler_params=pltpu.CompilerParams(dimension_semantics=("arbitrary",), vmem_limit_bytes=VMEM_LIMIT),
        name="inproj",
    )(x2d, w_pad, *tabs)


def _conv_kernel(u_ref, halo_ref, st_ref, w_ref, b_ref, g_ref, bt_ref, o_ref, f_ref, sh_ref, *, tc, rb):
    i = pl.program_id(1)
    f_ref[0:HALO, :] = jnp.where(i == 0, st_ref[0], halo_ref[0])
    f_ref[HALO:HALO + tc, :] = u_ref[0]
    span = tc + HALO - SUBLANES
    for c in range(1, SUBLANES):
        sh_ref[c, 0:span, :] = f_ref[c:c + span, :]
    lead = HALO - (CONV_WIDTH - 1)
    for r in range(tc // rb):
        acc = jnp.zeros((rb, C_CONV), f32)
        for j in range(CONV_WIDTH):
            c = (lead + j) % SUBLANES
            lo = r * rb + lead + j - c
            rows = f_ref[lo:lo + rb, :] if c == 0 else sh_ref[c, lo:lo + rb, :]
            acc = acc + rows * jnp.tile(w_ref[j], (rb // SUBLANES, 1))
        y = _layer_norm(acc + b_ref[...], g_ref[...], bt_ref[...])
        o_ref[0, r * rb:(r + 1) * rb, :] = (y * jax.nn.sigmoid(y)).astype(o_ref.dtype)


def _conv(u3, state, conv_w, conv_b, ln_g, ln_b):
    B, T, _ = u3.shape
    tc = min(256, T)
    st = jnp.pad(state.astype(f32), ((0, 0), (HALO - (CONV_WIDTH - 1), 0), (0, 0)))
    wp = jnp.broadcast_to(conv_w[:, None, :], (CONV_WIDTH, SUBLANES, C_CONV))
    vec = lambda a: a.reshape(1, C_CONV)
    hb = tc // HALO
    return pl.pallas_call(
        functools.partial(_conv_kernel, tc=tc, rb=32),
        grid=(B, T // tc),
        in_specs=[
            pl.BlockSpec((1, tc, C_CONV), lambda b, i: (b, i, 0)),
            pl.BlockSpec((1, HALO, C_CONV), lambda b, i: (b, jnp.maximum(i * hb - 1, 0), 0)),
            pl.BlockSpec((1, HALO, C_CONV), lambda b, i: (b, 0, 0)),
            pl.BlockSpec((CONV_WIDTH, SUBLANES, C_CONV), lambda b, i: (0, 0, 0)),
        ] + [pl.BlockSpec((1, C_CONV), lambda b, i: (0, 0))] * 3,
        out_specs=pl.BlockSpec((1, tc, C_CONV), lambda b, i: (b, i, 0)),
        out_shape=jax.ShapeDtypeStruct((B, T, C_CONV), bf16),
        scratch_shapes=[pltpu.VMEM((HALO + tc, C_CONV), f32),
                        pltpu.VMEM((SUBLANES, HALO + tc - SUBLANES, C_CONV), f32)],
        compiler_params=pltpu.CompilerParams(dimension_semantics=("arbitrary", "arbitrary"),
                                             vmem_limit_bytes=VMEM_LIMIT),
        name="conv",
    )(u3, u3, st, wp, vec(conv_b), vec(ln_g), vec(ln_b))


def _key_to_f32(key):
    bits = key ^ ((key >> 31) & jnp.int32(0x7FFFFFFF))
    return pltpu.bitcast(bits, f32)


def _attn_kernel(qi_ref, wi_ref, q_ref, ki_ref, k_ref, vt_ref, o_ref, sc_ref, m_ref, acc_ref, ot_ref, s_ref,
                 mx_ref, st_ref, planes_ref, cand_ref,
                 *, tq, hp, ts, t1, ta, past, topk):
    nvh = N_HEADS // hp
    lw = hp * tq
    lw2 = sc_ref.shape[1]
    j = pl.program_id(1)
    n_full = (past + j * tq) // ts
    n_chunks = n_full + 1
    kf = float(topk)

    q_chunk = (lax.broadcasted_iota(jnp.int32, (t1, lw2), 1) % tq) // CHUNK

    nc_max = cand_ref.shape[0]
    int_min = jnp.int32(INT_MIN)
    cand_ref[...] = jnp.zeros(cand_ref.shape, jnp.int32)

    def write_planes(c):
        row0 = pl.multiple_of(c * ts, ts)
        for lt in range(lw2 // LANES):
            lanes = slice(lt * LANES, (lt + 1) * LANES)
            bits = pltpu.bitcast(sc_ref[pl.ds(row0, ts), lanes], jnp.int32)
            u = bits ^ ((bits >> 31) | int_min)
            w = [u[i * SUBLANES:(i + 1) * SUBLANES, :] for i in range(32)]
            j, m = 16, 0x0000FFFF
            while j:
                k = 0
                while k < 32:
                    t = (w[k] ^ (w[k + j] >> j)) & jnp.int32(m)
                    w[k] = w[k] ^ t
                    w[k + j] = w[k + j] ^ (t << j)
                    k = (k + j + 1) & ~j
                j >>= 1
                m = m ^ (m << j)
            gone = jnp.full((SUBLANES, LANES), -1, jnp.int32)
            for b in range(32):
                plane = w[31 - b]
                planes_ref[b, c, :, lanes] = plane
                gone = gone & (plane if (NEG_INF_WORD >> b) & 1 else ~plane)
            cand_ref[c, :, lanes] = ~gone

    def score_chunk(c, masked):
        for r in range(ts // t1):
            row0 = pl.multiple_of(c * ts + r * t1, t1)
            kic = ki_ref[pl.ds(row0, t1), :]
            acc = jnp.zeros((t1, lw), f32)
            for hv in range(nvh):
                d = lax.dot_general(kic, qi_ref[hv], _NT, preferred_element_type=f32)
                acc = acc + jnp.maximum(d, 0.0) * wi_ref[0, hv:hv + 1, :]
            width = lw
            while width > tq:
                width //= 2
                acc = acc + pltpu.roll(acc, width, 1)
            acc = acc[:, :lw2]
            if masked:
                k_chunk = (lax.broadcasted_iota(jnp.int32, (t1, lw2), 0) + r * t1) // CHUNK
                acc = jnp.where(k_chunk <= q_chunk, acc, -jnp.inf)
            sc_ref[pl.ds(row0, t1), :] = acc
        write_planes(c)

    def pair_body(i, carry):
        score_chunk(2 * i, False)
        score_chunk(2 * i + 1, False)
        return carry

    lax.fori_loop(0, n_full // 2, pair_body, 0)

    @pl.when(n_full % 2 == 1)
    def _():
        score_chunk(n_full - 1, False)

    score_chunk(n_full, True)

    def count(preds):
        def body(c, accs):
            row0 = pl.multiple_of(c * ts, ts)
            blk = sc_ref[pl.ds(row0, ts), :]
            out = []
            for pred, acc in zip(preds, accs):
                m = jnp.where(pred(blk, row0), 1.0, 0.0)
                out.append(acc + m.reshape(ts // SUBLANES, SUBLANES, lw2).sum(axis=0))
            return tuple(out)
        init = tuple(jnp.zeros((SUBLANES, lw2), f32) for _ in preds)
        accs = lax.fori_loop(0, n_chunks, body, init)
        return [a.sum(axis=0, keepdims=True) for a in accs]

    def cut_counts(thr):
        c_ge, c_gt = count([lambda blk, row0: blk >= thr, lambda blk, row0: blk > thr])
        st_ref[0:1, :] = thr
        st_ref[1:2, :] = c_ge
        st_ref[2:3, :] = c_gt

    def idle_body(c, carry):
        planes_ref[:, c] = jnp.zeros((32, SUBLANES, lw2), jnp.int32)
        return carry

    lax.fori_loop(n_chunks, nc_max, idle_body, 0)

    def bit_count(b, flip):
        cnt = jnp.zeros((SUBLANES, lw2), jnp.int32)
        tot = jnp.zeros((SUBLANES, lw2), jnp.int32)
        for c in range(nc_max):
            cand = cand_ref[c]
            if flip is None:
                tot = tot + lax.population_count(cand)
            else:
                cand = cand & (planes_ref[b + 1, c] ^ flip)
                cand_ref[c] = cand
            cnt = cnt + lax.population_count(cand & planes_ref[b, c])
        red = lambda a: a.astype(f32).sum(axis=0, keepdims=True)
        return red(cnt), red(tot)

    def decide(c1, left, word, b):
        take = c1 >= left
        return (jnp.where(take, left, left - c1), word | jnp.where(take, lax.shift_left(jnp.int32(1), b), 0),
                jnp.where(take, 0, -1))

    c1, n_adm = bit_count(31, None)
    state = decide(c1, jnp.full((1, lw2), kf, f32), jnp.zeros((1, lw2), jnp.int32), 31)

    def radix_body(it, state):
        left, word, flip = state
        b = 30 - it
        c1, _ = bit_count(b, flip)
        return decide(c1, left, word, b)

    _, word, _ = lax.fori_loop(0, 31, radix_body, state)
    few = n_adm < kf
    cut_counts(jnp.where(few, F32_LOWEST, _key_to_f32(word ^ int_min)))

    good = jnp.logical_or(few, jnp.logical_and(st_ref[2:3, :] < kf, st_ref[1:2, :] >= kf))

    @pl.when(jnp.max(jnp.where(good, 0.0, 1.0)) > 0.0)
    def _():
        def bisect(it, key):
            cand = key + lax.shift_left(jnp.int32(1), 31 - it)
            thr = _key_to_f32(cand)
            cnt, = count([lambda blk, row0: blk >= thr])
            return jnp.where(cnt >= kf, cand, key)

        key = lax.fori_loop(0, 32, bisect, jnp.full((1, lw2), INT_MIN, jnp.int32))
        cut_counts(jnp.where(key == INT_MIN, F32_LOWEST, _key_to_f32(key)))

    thr, c_ge, c_gt = st_ref[0:1, :], st_ref[1:2, :], st_ref[2:3, :]
    ties = c_ge > kf
    need = kf - c_gt
    any_ties = jnp.max(jnp.where(ties, 1.0, 0.0))

    def write_bias(select):
        def body(c, carry):
            row0 = pl.multiple_of(c * ts, ts)
            blk = sc_ref[pl.ds(row0, ts), :]
            sc_ref[pl.ds(row0, ts), :] = jnp.where(select(blk, row0), 0.0, NEG_BIG)
            return carry
        lax.fori_loop(0, n_chunks, body, 0)

    @pl.when(any_ties == 0.0)
    def _():
        write_bias(lambda blk, row0: blk >= thr)

    @pl.when(any_ties > 0.0)
    def _():
        def rows(row0):
            return lax.broadcasted_iota(jnp.int32, (ts, lw2), 0) + row0

        tb = pltpu.bitcast(thr, jnp.int32)
        tw = tb ^ ((tb >> 31) | int_min)
        cand_ref[...] = jnp.full(cand_ref.shape, -1, jnp.int32)

        def tied_body(b, carry):
            flip = jnp.where(((tw >> b) & 1) == 1, 0, -1)
            for c in range(nc_max):
                cand_ref[c] = cand_ref[c] & (planes_ref[b, c] ^ flip)
            return carry

        lax.fori_loop(0, 32, tied_body, 0)

        sub = lax.broadcasted_iota(jnp.int32, (SUBLANES, lw2), 0)
        passes = [("chunk", k) for k in reversed(range(max(1, (nc_max - 1).bit_length())))]
        passes += [("group", k) for k in reversed(range(5))] + [("sublane", k) for k in reversed(range(3))]
        left, cut = need, jnp.zeros((1, lw2), jnp.int32)
        for kind, k in passes:
            def high(c, kind=kind, k=k):
                if kind == "chunk":
                    return jnp.int32(-1 if (c >> k) & 1 else 0)
                if kind == "group":
                    word = sum(1 << i for i in range(32) if ((31 - i) >> k) & 1)
                    return jnp.int32(word - (1 << 32) if word >= (1 << 31) else word)
                return jnp.where(((sub >> k) & 1) == 1, -1, 0)
            cnt = jnp.zeros((SUBLANES, lw2), jnp.int32)
            for c in range(nc_max):
                cnt = cnt + lax.population_count(cand_ref[c] & ~high(c))
            c0 = cnt.astype(f32).sum(axis=0, keepdims=True)
            low = c0 >= left
            left = jnp.where(low, left, left - c0)
            weight = {"chunk": 8, "group": 3, "sublane": 0}[kind] + k
            cut = cut | jnp.where(low, 0, 1 << weight)
            flip = jnp.where(low, -1, 0)
            for c in range(nc_max):
                cand_ref[c] = cand_ref[c] & (high(c) ^ flip)
        cut = jnp.where(ties, cut, jnp.int32(2 ** 30))
        write_bias(lambda blk, row0: jnp.logical_or(
            blk > thr, jnp.logical_and(blk == thr, rows(row0) <= cut)))

    m_ref[...] = jnp.full(m_ref.shape, NEG_BIG, f32)
    acc_ref[...] = jnp.zeros(acc_ref.shape, f32)

    n_steps = n_chunks * (ts // ta)
    last = n_steps - 1

    def logits(step, buf):
        row0 = pl.multiple_of(step * ta, ta)
        kc = k_ref[pl.ds(row0, ta), :]
        for hv in range(nvh):
            bias = sc_ref[pl.ds(row0, ta), :]
            if lw2 < lw:
                bias = jnp.concatenate([bias] * (lw // lw2), axis=1)
            s = lax.dot_general(kc, q_ref[hv], _NT, preferred_element_type=f32) + bias
            s_ref[buf, hv] = s
            mx_ref[buf, hv] = s.reshape(ta // SUBLANES, SUBLANES, lw).max(axis=0).max(axis=0, keepdims=True)

    def absorb(step, buf):
        row0 = pl.multiple_of(step * ta, ta)
        for hv in range(nvh):
            g = (hv * hp) // GROUP
            m_old = m_ref[hv]
            m_new = jnp.maximum(m_old, mx_ref[buf, hv])
            alpha = jnp.exp2(m_old - m_new)
            p = jnp.exp2(s_ref[buf, hv] - m_new).astype(bf16)
            vt = vt_ref[g, :, pl.ds(row0, ta)]
            acc_ref[hv] = alpha * acc_ref[hv] + jnp.dot(vt, p, preferred_element_type=f32)
            m_ref[hv] = m_new

    logits(0, 0)

    def att_body(i, carry):
        logits(jnp.minimum(2 * i + 1, last), 1)
        absorb(2 * i, 0)
        logits(jnp.minimum(2 * i + 2, last), 0)
        absorb(2 * i + 1, 1)
        return carry

    lax.fori_loop(0, n_steps // 2, att_body, 0)

    @pl.when(n_steps % 2 == 1)
    def _():
        absorb(last, 0)

    for hv in range(nvh):
        o = acc_ref[hv]
        ot_ref[hv * HEAD_DIM:(hv + 1) * HEAD_DIM, :] = o[:HEAD_DIM] / o[HEAD_DIM:HEAD_DIM + 1]
    if hp == 1:
        o_ref[...] = ot_ref[...].T.astype(o_ref.dtype)
    else:
        o_ref[0] = ot_ref[...].astype(o_ref.dtype)


def _attention(qi_hm, wi2d, q_hm, ki_all, k_all, vt_all, *, B, T, past, topk):
    tq = min(256, T)
    ts = 256
    ta = 256
    hp = ts // tq
    assert past % ts == 0 and (tq == ts or tq == T) and tq % CHUNK == 0 and hp in (1, 2, 4)
    nvh = N_HEADS // hp
    lw = hp * tq
    lw2 = max(tq, LANES)
    nq = T // tq
    nb = B * nq
    s_pad = (past + T + ts - 1) // ts * ts
    assert ki_all.shape == (B * s_pad, LANES) and k_all.shape == (B * s_pad, LANES)
    assert vt_all.shape == (N_KV_HEADS, V_ROWS, B * s_pad)
    N = B * T

    def pack(a):
        if hp == 1:
            return a
        a = a.reshape(nvh, hp, nb, tq, LANES)
        return jnp.transpose(a, (0, 2, 1, 3, 4)).reshape(nvh, nb * lw, LANES)

    wi3 = jnp.transpose(wi2d.reshape(nb, tq, nvh, hp), (0, 2, 3, 1)).reshape(nb, nvh, lw)
    hm_spec = pl.BlockSpec((nvh, lw, LANES), lambda b, j: (0, b * nq + j, 0))
    if hp == 1:
        out_spec = pl.BlockSpec((tq, ATTN_WIDTH), lambda b, j: (b * nq + j, 0))
        out_shape = jax.ShapeDtypeStruct((N, ATTN_WIDTH), bf16)
    else:
        out_spec = pl.BlockSpec((1, nvh * HEAD_DIM, lw), lambda b, j: (b * nq + j, 0, 0))
        out_shape = jax.ShapeDtypeStruct((nb, nvh * HEAD_DIM, lw), bf16)
    out = pl.pallas_call(
        functools.partial(_attn_kernel, tq=tq, hp=hp, ts=ts, t1=128, ta=ta, past=past, topk=topk),
        grid=(B, nq),
        in_specs=[
            hm_spec,
            pl.BlockSpec((1, nvh, lw), lambda b, j: (b * nq + j, 0, 0)),
            hm_spec,
            pl.BlockSpec((s_pad, LANES), lambda b, j: (b, 0)),
            pl.BlockSpec((s_pad, LANES), lambda b, j: (b, 0)),
            pl.BlockSpec((N_KV_HEADS, V_ROWS, s_pad), lambda b, j: (0, 0, b)),
        ],
        out_specs=out_spec,
        out_shape=out_shape,
        scratch_shapes=[pltpu.VMEM((s_pad, lw2), f32), pltpu.VMEM((nvh, 1, lw), f32),
                        pltpu.VMEM((nvh, V_ROWS, lw), f32), pltpu.VMEM((nvh * HEAD_DIM, lw), f32),
                        pltpu.VMEM((2, nvh, ta, lw), f32), pltpu.VMEM((2, nvh, 1, lw), f32),
                        pltpu.VMEM((SUBLANES, lw2), f32),
                        pltpu.VMEM((32, s_pad // ts, SUBLANES, lw2), jnp.int32),
                        pltpu.VMEM((s_pad // ts, SUBLANES, lw2), jnp.int32)],
        compiler_params=pltpu.CompilerParams(dimension_semantics=("arbitrary", "arbitrary"),
                                             vmem_limit_bytes=VMEM_LIMIT),
        name="dsa_attention",
    )(pack(qi_hm), wi3, pack(q_hm), ki_all, k_all, vt_all)
    if hp > 1:
        out = out.reshape(nb, nvh, HEAD_DIM, hp, tq)
        out = jnp.transpose(out, (0, 4, 1, 3, 2)).reshape(N, ATTN_WIDTH)
    return out


FF_CHUNK = 256


def _post_kernel(x_ref, c_ref, a_ref, wo_ref, g1_ref, b1_ref, wgu_ref, wd_ref, g2_ref, b2_ref, o_ref, acc_ref):
    w16 = lambda r, rows, cols: r[0, rows, cols].astype(bf16)
    full = slice(None)
    mix = (jnp.dot(c_ref[...], w16(wo_ref, slice(0, C_CONV), full), preferred_element_type=f32)
           + jnp.dot(a_ref[...], w16(wo_ref, slice(C_CONV, D_MODEL), full), preferred_element_type=f32))
    x1 = _layer_norm(ALPHA * x_ref[...] + mix, g1_ref[...], b1_ref[...])
    x1b = x1.astype(bf16)
    for c in range(D_FF // FF_CHUNK):
        lo = c * FF_CHUNK
        gate = jnp.dot(x1b, w16(wgu_ref, full, slice(lo, lo + FF_CHUNK)), preferred_element_type=f32)
        up = jnp.dot(x1b, w16(wgu_ref, full, slice(D_FF + lo, D_FF + lo + FF_CHUNK)), preferred_element_type=f32)
        act = (gate * jax.nn.sigmoid(gate) * up).astype(bf16)
        part = jnp.dot(act, w16(wd_ref, slice(lo, lo + FF_CHUNK), full), preferred_element_type=f32)
        if c == 0:
            acc_ref[...] = part
        else:
            acc_ref[...] += part
    o_ref[...] = _layer_norm(ALPHA * x1 + acc_ref[...], g2_ref[...], b2_ref[...])


def _post(x2d, conv2d, attn2d, layer, w_o, g1, b1, w_gate_up, w_down, g2, b2):
    N = x2d.shape[0]
    tm = min(512, N)
    row = lambda w: pl.BlockSpec((tm, w), lambda i: (i, 0))
    vec_spec = pl.BlockSpec((1, D_MODEL), lambda i: (0, 0))
    weight = lambda a: pl.BlockSpec((1,) + a.shape[1:], lambda i: (layer, 0, 0), pipeline_mode=pl.Buffered(1))
    vec = lambda a: a.reshape(1, D_MODEL)
    return pl.pallas_call(
        _post_kernel,
        grid=(N // tm,),
        in_specs=[row(D_MODEL), row(C_CONV), row(ATTN_WIDTH), weight(w_o), vec_spec, vec_spec,
                  weight(w_gate_up), weight(w_down), vec_spec, vec_spec],
        out_specs=row(D_MODEL),
        out_shape=jax.ShapeDtypeStruct((N, D_MODEL), f32),
        scratch_shapes=[pltpu.VMEM((tm, D_MODEL), f32)],
        compiler_params=pltpu.CompilerParams(dimension_semantics=("arbitrary",), vmem_limit_bytes=VMEM_LIMIT),
        name="post",
    )(x2d, conv2d, attn2d, w_o, vec(g1), vec(b1), w_gate_up, w_down, vec(g2), vec(b2))


def _layer(x, conv_state, k_past, v_past, ki_past, wts):
    (layer, w_pad, conv_w, conv_b, cln_g, cln_b, w_o, ln1_g, ln1_b, w_gate_up, w_down, ln2_g, ln2_b) = wts
    B, T, _ = x.shape
    past = k_past.shape[1]
    N = B * T
    x2d = x.reshape(N, D_MODEL)
    u, q_hm, k2d, v2d, qi_hm, kw, kb, kwb, vt = _inproj(x2d, w_pad, T, past)

    u3 = u.reshape(B, T, C_CONV)
    conv_out = _conv(u3, conv_state, conv_w, conv_b, cln_g, cln_b)
    new_conv = jnp.concatenate([conv_state.astype(f32), u3], axis=1)[:, -(CONV_WIDTH - 1):]

    k = k2d.reshape(B, T, N_KV_HEADS, HEAD_DIM)
    v = v2d.reshape(B, T, N_KV_HEADS, HEAD_DIM)
    ki = kw[:, :IDX_DIM].reshape(B, T, IDX_DIM)
    S = past + T
    s_pad = -(-S // 256) * 256
    if past == 0 and s_pad == T:
        ki_all, k_all, vt_all = kwb, kb, vt
    else:
        def key_rows(old, new):
            a = jnp.concatenate([old, new.reshape(B, T, LANES)], axis=1)
            return jnp.pad(a, ((0, 0), (0, s_pad - S), (0, 0))).reshape(B * s_pad, LANES)

        ki_all = key_rows(jnp.pad(ki_past.astype(bf16), ((0, 0), (0, 0), (0, LANES - IDX_DIM))), kwb)
        k_all = key_rows(k_past.astype(bf16).reshape(B, past, LANES), kb)
        extra = (jnp.arange(V_ROWS - HEAD_DIM) == 0).astype(bf16)
        vt_old = jnp.concatenate([
            jnp.transpose(v_past.astype(bf16), (2, 3, 0, 1)),
            jnp.broadcast_to(extra[None, :, None, None], (N_KV_HEADS, V_ROWS - HEAD_DIM, B, past))], axis=1)
        vt_all = jnp.concatenate([vt_old, vt.reshape(N_KV_HEADS, V_ROWS, B, T)], axis=3)
        vt_all = jnp.pad(vt_all, ((0, 0), (0, 0), (0, 0), (0, s_pad - S))).reshape(N_KV_HEADS, V_ROWS, B * s_pad)
    attn = _attention(qi_hm, kw[:, IDX_DIM:IDX_DIM + N_IDX_HEADS], q_hm, ki_all, k_all, vt_all,
                      B=B, T=T, past=past, topk=min(TOPK_MAX, S // 4))

    y = _post(x2d, conv_out.reshape(N, C_CONV), attn, layer, w_o, ln1_g, ln1_b, w_gate_up, w_down, ln2_g, ln2_b)
    return y.reshape(B, T, D_MODEL), k, v, ki, new_conv


def kernel(x_prompt, x_sample, cache_k, cache_v, cache_k_idx, state_conv, w_in, conv_w, conv_b, conv_ln_g,
           conv_ln_b, w_o, ln1_g, ln1_b, w_gate_up, w_down, ln2_g, ln2_b):
    Bp = x_prompt.shape[0]
    dt = x_prompt.dtype
    empty_kv = jnp.zeros((Bp, 0, N_KV_HEADS, HEAD_DIM), dt)
    empty_ki = jnp.zeros((Bp, 0, IDX_DIM), dt)
    zero_conv = jnp.zeros((Bp, CONV_WIDTH - 1, C_CONV), dt)
    hp, hs = x_prompt, x_sample
    outs_p, outs_s = [], []
    for l in range(DEPTH):
        wts = (
            l, jnp.pad(w_in[l].astype(bf16), ((0, 0), (0, IN_PAD - IN_DIM))),
            conv_w[l], conv_b[l], conv_ln_g[l], conv_ln_b[l],
            w_o, ln1_g[l], ln1_b[l], w_gate_up, w_down, ln2_g[l], ln2_b[l],
        )
        hp, *op = _layer(hp, zero_conv, empty_kv, empty_kv, empty_ki, wts)
        hs, *os_ = _layer(hs, state_conv[l], cache_k[l], cache_v[l], cache_k_idx[l], wts)
        outs_p.append(op)
        outs_s.append(os_)
    stack = lambda outs, i: jnp.stack([o[i] for o in outs])
    return (hp, hs,
            stack(outs_p, 0), stack(outs_p, 1), stack(outs_p, 2), stack(outs_p, 3),
            stack(outs_s, 0), stack(outs_s, 1), stack(outs_s, 2), stack(outs_s, 3))
```

```python
import functools
import math

import jax
import jax.numpy as jnp
from jax import lax
from jax.experimental import pallas as pl
from jax.experimental.pallas import tpu as pltpu

D_MODEL = 1024
CHUNK = 64
C_CONV = D_MODEL // 2
CONV_WIDTH = 31
HEAD_DIM = 64
N_HEADS = (D_MODEL // 2) // HEAD_DIM
N_KV_HEADS = 2
GROUP = N_HEADS // N_KV_HEADS
ATTN_WIDTH = N_HEADS * HEAD_DIM
N_IDX_HEADS = 8
IDX_DIM = 64
TOPK_MAX = 256
ROPE_THETA = 10000.0
D_FF = -(-8 * D_MODEL // (3 * 256)) * 256
IN_DIM = 2 * C_CONV + (N_HEADS + 2 * N_KV_HEADS) * HEAD_DIM + N_IDX_HEADS * IDX_DIM + IDX_DIM + N_IDX_HEADS
DEPTH = 2
ALPHA = (2 * DEPTH) ** 0.25
ATTN_SCALE = HEAD_DIM ** -0.5
IDX_SCALE = IDX_DIM ** -0.5
IDX_HEAD_SCALE = N_IDX_HEADS ** -0.5
LN_EPS = 1e-5
Q_SCALE = ATTN_SCALE * math.log2(math.e)

LANES = 128
SUBLANES = 8
VMEM_LIMIT = 56 * 1024 * 1024
IN_PAD = -(-IN_DIM // LANES) * LANES
HALO = 32
NEG_BIG = -1e30
INT_MIN = -(2 ** 31)
F32_LOWEST = float(jnp.finfo(jnp.float32).min)
NEG_INF_WORD = 0x007FFFFF
V_ROWS = HEAD_DIM + 16

_O_A = 0
_O_G = _O_A + C_CONV
_O_Q = _O_G + C_CONV
_O_K = _O_Q + ATTN_WIDTH
_O_V = _O_K + N_KV_HEADS * HEAD_DIM
_O_QI = _O_V + N_KV_HEADS * HEAD_DIM
_O_KW = _O_QI + N_IDX_HEADS * IDX_DIM

_NT = (((1,), (1,)), ((), ()))

f32 = jnp.float32
bf16 = jnp.bfloat16


def _layer_norm(y, g, b):
    mu = jnp.mean(y, axis=-1, keepdims=True)
    d = y - mu
    var = jnp.mean(d * d, axis=-1, keepdims=True)
    return d * lax.rsqrt(var + LN_EPS) * g + b


def _rope128(x, cos, sa, sb):
    return x * cos + pltpu.roll(x, 96, 1) * sa + pltpu.roll(x, 32, 1) * sb


def _inproj_kernel(x_ref, w_ref, wt_ref, cos_ref, sa_ref, sb_ref, cosk_ref, sak_ref, sbk_ref,
                   u_ref, q_ref, qi_ref, kt_ref, vt_ref, kwt_ref, *, seq):
    xb = x_ref[...].astype(bf16)
    tm = xb.shape[0]

    def put_t(ref, a):
        at = a.T
        for s in range(ref.shape[0]):
            ref[s] = at[:, s * seq:(s + 1) * seq] if tm > seq else at

    def mm(lo, width):
        return jnp.dot(xb, w_ref[0, :, lo:lo + width].astype(bf16), preferred_element_type=f32)

    u_ref[...] = mm(_O_A, C_CONV) * jax.nn.sigmoid(mm(_O_G, C_CONV))

    cos, sa, sb = cos_ref[...], sa_ref[...], sb_ref[...]
    low = lax.broadcasted_iota(jnp.int32, (tm, LANES), 1) < HEAD_DIM

    def place(r, src, dst):
        if src != dst:
            r = pltpu.roll(r, HEAD_DIM, 1)
        return jnp.where(low if dst == 0 else jnp.logical_not(low), r, 0.0).astype(bf16)

    q = mm(_O_Q, ATTN_WIDTH)
    qi = mm(_O_QI, N_IDX_HEADS * IDX_DIM)
    for c in range(ATTN_WIDTH // LANES):
        r = _rope128(q[:, c * LANES:(c + 1) * LANES], cos, sa, sb) * Q_SCALE
        for half in range(2):
            h = 2 * c + half
            q_ref[h] = place(r, half, h // GROUP)
        r = _rope128(qi[:, c * LANES:(c + 1) * LANES], cos, sa, sb)
        for half in range(2):
            qi_ref[2 * c + half] = place(r, half, 0)
    put_t(kt_ref, _rope128(mm(_O_K, N_KV_HEADS * HEAD_DIM), cos, sa, sb))
    put_t(vt_ref, mm(_O_V, N_KV_HEADS * HEAD_DIM))
    kw = jnp.dot(xb, wt_ref[...], preferred_element_type=f32)
    put_t(kwt_ref, _rope128(kw, cosk_ref[...], sak_ref[...], sbk_ref[...]))


def _rope_tables(T, past, rows):
    half = HEAD_DIM // 2
    inv = ROPE_THETA ** (-jnp.arange(half, dtype=f32) / half)
    pos = (jnp.arange(T, dtype=jnp.int32) + past).astype(f32)
    ang = pos[:, None] * inv[None, :]
    cos, sin = jnp.cos(ang), jnp.sin(ang)
    zero = jnp.zeros_like(sin)
    cos64 = jnp.concatenate([cos, cos], axis=1)
    sa64 = jnp.concatenate([-sin, zero], axis=1)
    sb64 = jnp.concatenate([zero, sin], axis=1)
    two = lambda a: jnp.concatenate([a, a], axis=1)
    wscale = jnp.full((T, N_IDX_HEADS), IDX_HEAD_SCALE * IDX_SCALE, f32)
    ones = jnp.ones((T, LANES - IDX_DIM - N_IDX_HEADS), f32)
    cosk = jnp.concatenate([cos64, wscale, ones], axis=1)
    zpad = jnp.zeros((T, LANES - IDX_DIM), f32)
    tabs = (two(cos64), two(sa64), two(sb64), cosk,
            jnp.concatenate([sa64, zpad], axis=1), jnp.concatenate([sb64, zpad], axis=1))
    reps = rows // T
    return tuple(jnp.tile(t, (reps, 1)) for t in tabs)


def _inproj(x2d, layer, w_in, w_tail, T, past):
    N = x2d.shape[0]
    tm = min(512, N)
    rows = max(T, tm)
    tabs = _rope_tables(T, past, rows)
    nt = rows // tm
    row_spec = lambda w: pl.BlockSpec((tm, w), lambda i: (i, 0))
    tab_spec = pl.BlockSpec((tm, LANES), lambda i: (i % nt, 0))
    hm_spec = pl.BlockSpec((N_HEADS, tm, LANES), lambda i: (0, i, 0))
    if tm <= T:
        per = T // tm
        t_spec = pl.BlockSpec((1, LANES, tm), lambda i: (i // per, 0, i % per))
    else:
        t_spec = pl.BlockSpec((tm // T, LANES, T), lambda i: (i, 0, 0))
    t_shape = jax.ShapeDtypeStruct((N // T, LANES, T), f32)
    return pl.pallas_call(
        functools.partial(_inproj_kernel, seq=T),
        grid=(N // tm,),
        in_specs=[row_spec(D_MODEL),
                  pl.BlockSpec((1, D_MODEL, IN_DIM), lambda i: (layer, 0, 0), pipeline_mode=pl.Buffered(1)),
                  pl.BlockSpec((D_MODEL, LANES), lambda i: (0, 0))] + [tab_spec] * 6,
        out_specs=[row_spec(C_CONV), hm_spec, hm_spec, t_spec, t_spec, t_spec],
        out_shape=[
            jax.ShapeDtypeStruct((N, C_CONV), f32),
            jax.ShapeDtypeStruct((N_HEADS, N, LANES), bf16),
            jax.ShapeDtypeStruct((N_IDX_HEADS, N, LANES), bf16),
            t_shape, t_shape, t_shape,
        ],
        compiler_params=pltpu.CompilerParams(dimension_semantics=("arbitrary",), vmem_limit_bytes=VMEM_LIMIT),
        name="inproj",
    )(x2d, w_in, w_tail, *tabs)


def _conv_kernel(u_ref, halo_ref, st_ref, w_ref, b_ref, g_ref, bt_ref, o_ref, f_ref, sh_ref, *, tc, rb):
    i = pl.program_id(1)
    f_ref[0:HALO, :] = jnp.where(i == 0, st_ref[0], halo_ref[0])
    f_ref[HALO:HALO + tc, :] = u_ref[0]
    span = tc + HALO - SUBLANES
    for c in range(1, SUBLANES):
        sh_ref[c, 0:span, :] = f_ref[c:c + span, :]
    lead = HALO - (CONV_WIDTH - 1)
    for r in range(tc // rb):
        acc = jnp.zeros((rb, C_CONV), f32)
        for j in range(CONV_WIDTH):
            c = (lead + j) % SUBLANES
            lo = r * rb + lead + j - c
            rows = f_ref[lo:lo + rb, :] if c == 0 else sh_ref[c, lo:lo + rb, :]
            acc = acc + rows * jnp.tile(w_ref[j], (rb // SUBLANES, 1))
        y = _layer_norm(acc + b_ref[...], g_ref[...], bt_ref[...])
        o_ref[0, r * rb:(r + 1) * rb, :] = (y * jax.nn.sigmoid(y)).astype(o_ref.dtype)


def _conv(u3, state, conv_w, conv_b, ln_g, ln_b):
    B, T, _ = u3.shape
    tc = min(256, T)
    st = jnp.pad(state.astype(f32), ((0, 0), (HALO - (CONV_WIDTH - 1), 0), (0, 0)))
    wp = jnp.broadcast_to(conv_w[:, None, :], (CONV_WIDTH, SUBLANES, C_CONV))
    vec = lambda a: a.reshape(1, C_CONV)
    hb = tc // HALO
    return pl.pallas_call(
        functools.partial(_conv_kernel, tc=tc, rb=32),
        grid=(B, T // tc),
        in_specs=[
            pl.BlockSpec((1, tc, C_CONV), lambda b, i: (b, i, 0)),
            pl.BlockSpec((1, HALO, C_CONV), lambda b, i: (b, jnp.maximum(i * hb - 1, 0), 0)),
            pl.BlockSpec((1, HALO, C_CONV), lambda b, i: (b, 0, 0)),
            pl.BlockSpec((CONV_WIDTH, SUBLANES, C_CONV), lambda b, i: (0, 0, 0)),
        ] + [pl.BlockSpec((1, C_CONV), lambda b, i: (0, 0))] * 3,
        out_specs=pl.BlockSpec((1, tc, C_CONV), lambda b, i: (b, i, 0)),
        out_shape=jax.ShapeDtypeStruct((B, T, C_CONV), bf16),
        scratch_shapes=[pltpu.VMEM((HALO + tc, C_CONV), f32),
                        pltpu.VMEM((SUBLANES, HALO + tc - SUBLANES, C_CONV), f32)],
        compiler_params=pltpu.CompilerParams(dimension_semantics=("arbitrary", "arbitrary"),
                                             vmem_limit_bytes=VMEM_LIMIT),
        name="conv",
    )(u3, u3, st, wp, vec(conv_b), vec(ln_g), vec(ln_b))


def _key_to_f32(key):
    bits = key ^ ((key >> 31) & jnp.int32(0x7FFFFFFF))
    return pltpu.bitcast(bits, f32)


def _attn_kernel(qi_ref, wi_ref, q_ref, kwt_ref, kt_ref, vt_ref, o_ref, sc_ref, m_ref, acc_ref, ot_ref, s_ref,
                 mx_ref, st_ref, planes_ref, cand_ref,
                 *, tq, hp, ts, t1, ta, past, topk):
    nvh = N_HEADS // hp
    lw = hp * tq
    lw2 = sc_ref.shape[1]
    j = pl.program_id(1)
    n_full = (past + j * tq) // ts
    n_chunks = n_full + 1
    kf = float(topk)

    def key_rows(ref, row0, n):
        return ref[0, :, pl.ds(row0, n)].astype(f32).T.astype(bf16)

    ones_rows = jnp.where(lax.broadcasted_iota(jnp.int32, (V_ROWS - HEAD_DIM, ta), 0) == 0, 1.0, 0.0).astype(bf16)

    q_chunk = (lax.broadcasted_iota(jnp.int32, (t1, lw2), 1) % tq) // CHUNK

    nc_max = cand_ref.shape[0]
    int_min = jnp.int32(INT_MIN)
    cand_ref[...] = jnp.zeros(cand_ref.shape, jnp.int32)

    def write_planes(c):
        row0 = pl.multiple_of(c * ts, ts)
        for lt in range(lw2 // LANES):
            lanes = slice(lt * LANES, (lt + 1) * LANES)
            bits = pltpu.bitcast(sc_ref[pl.ds(row0, ts), lanes], jnp.int32)
            u = bits ^ ((bits >> 31) | int_min)
            w = [u[i * SUBLANES:(i + 1) * SUBLANES, :] for i in range(32)]
            j, m = 16, 0x0000FFFF
            while j:
                k = 0
                while k < 32:
                    t = (w[k] ^ (w[k + j] >> j)) & jnp.int32(m)
                    w[k] = w[k] ^ t
                    w[k + j] = w[k + j] ^ (t << j)
                    k = (k + j + 1) & ~j
                j >>= 1
                m = m ^ (m << j)
            gone = jnp.full((SUBLANES, LANES), -1, jnp.int32)
            for b in range(32):
                plane = w[31 - b]
                planes_ref[b, c, :, lanes] = plane
                gone = gone & (plane if (NEG_INF_WORD >> b) & 1 else ~plane)
            cand_ref[c, :, lanes] = ~gone

    def score_chunk(c, masked):
        for r in range(ts // t1):
            row0 = pl.multiple_of(c * ts + r * t1, t1)
            kic = key_rows(kwt_ref, row0, t1)
            acc = jnp.zeros((t1, lw), f32)
            for hv in range(nvh):
                d = lax.dot_general(kic, qi_ref[hv], _NT, preferred_element_type=f32)
                acc = acc + jnp.maximum(d, 0.0) * wi_ref[0, hv:hv + 1, :]
            width = lw
            while width > tq:
                width //= 2
                acc = acc + pltpu.roll(acc, width, 1)
            acc = acc[:, :lw2]
            if masked:
                k_chunk = (lax.broadcasted_iota(jnp.int32, (t1, lw2), 0) + r * t1) // CHUNK
                acc = jnp.where(k_chunk <= q_chunk, acc, -jnp.inf)
            sc_ref[pl.ds(row0, t1), :] = acc
        write_planes(c)

    def pair_body(i, carry):
        score_chunk(2 * i, False)
        score_chunk(2 * i + 1, False)
        return carry

    lax.fori_loop(0, n_full // 2, pair_body, 0)

    @pl.when(n_full % 2 == 1)
    def _():
        score_chunk(n_full - 1, False)

    score_chunk(n_full, True)

    def count(preds):
        def body(c, accs):
            row0 = pl.multiple_of(c * ts, ts)
            blk = sc_ref[pl.ds(row0, ts), :]
            out = []
            for pred, acc in zip(preds, accs):
                m = jnp.where(pred(blk, row0), 1.0, 0.0)
                out.append(acc + m.reshape(ts // SUBLANES, SUBLANES, lw2).sum(axis=0))
            return tuple(out)
        init = tuple(jnp.zeros((SUBLANES, lw2), f32) for _ in preds)
        accs = lax.fori_loop(0, n_chunks, body, init)
        return [a.sum(axis=0, keepdims=True) for a in accs]

    def cut_counts(thr):
        c_ge, c_gt = count([lambda blk, row0: blk >= thr, lambda blk, row0: blk > thr])
        st_ref[0:1, :] = thr
        st_ref[1:2, :] = c_ge
        st_ref[2:3, :] = c_gt

    def idle_body(c, carry):
        planes_ref[:, c] = jnp.zeros((32, SUBLANES, lw2), jnp.int32)
        return carry

    lax.fori_loop(n_chunks, nc_max, idle_body, 0)

    def bit_count(b, flip):
        cnt = jnp.zeros((SUBLANES, lw2), jnp.int32)
        tot = jnp.zeros((SUBLANES, lw2), jnp.int32)
        for c in range(nc_max):
            cand = cand_ref[c]
            if flip is None:
                tot = tot + lax.population_count(cand)
            else:
                cand = cand & (planes_ref[b + 1, c] ^ flip)
                cand_ref[c] = cand
            cnt = cnt + lax.population_count(cand & planes_ref[b, c])
        red = lambda a: a.astype(f32).sum(axis=0, keepdims=True)
        return red(cnt), red(tot)

    def decide(c1, left, word, b):
        take = c1 >= left
        return (jnp.where(take, left, left - c1), word | jnp.where(take, lax.shift_left(jnp.int32(1), b), 0),
                jnp.where(take, 0, -1))

    c1, n_adm = bit_count(31, None)
    state = decide(c1, jnp.full((1, lw2), kf, f32), jnp.zeros((1, lw2), jnp.int32), 31)

    def radix_body(it, state):
        left, word, flip = state
        b = 30 - it
        c1, _ = bit_count(b, flip)
        return decide(c1, left, word, b)

    _, word, _ = lax.fori_loop(0, 31, radix_body, state)
    few = n_adm < kf
    cut_counts(jnp.where(few, F32_LOWEST, _key_to_f32(word ^ int_min)))

    good = jnp.logical_or(few, jnp.logical_and(st_ref[2:3, :] < kf, st_ref[1:2, :] >= kf))

    @pl.when(jnp.max(jnp.where(good, 0.0, 1.0)) > 0.0)
    def _():
        def bisect(it, key):
            cand = key + lax.shift_left(jnp.int32(1), 31 - it)
            thr = _key_to_f32(cand)
            cnt, = count([lambda blk, row0: blk >= thr])
            return jnp.where(cnt >= kf, cand, key)

        key = lax.fori_loop(0, 32, bisect, jnp.full((1, lw2), INT_MIN, jnp.int32))
        cut_counts(jnp.where(key == INT_MIN, F32_LOWEST, _key_to_f32(key)))

    thr, c_ge, c_gt = st_ref[0:1, :], st_ref[1:2, :], st_ref[2:3, :]
    ties = c_ge > kf
    need = kf - c_gt
    any_ties = jnp.max(jnp.where(ties, 1.0, 0.0))

    def write_bias(select):
        def body(c, carry):
            row0 = pl.multiple_of(c * ts, ts)
            blk = sc_ref[pl.ds(row0, ts), :]
            sc_ref[pl.ds(row0, ts), :] = jnp.where(select(blk, row0), 0.0, NEG_BIG)
            return carry
        lax.fori_loop(0, n_chunks, body, 0)

    @pl.when(any_ties == 0.0)
    def _():
        write_bias(lambda blk, row0: blk >= thr)

    @pl.when(any_ties > 0.0)
    def _():
        def rows(row0):
            return lax.broadcasted_iota(jnp.int32, (ts, lw2), 0) + row0

        tb = pltpu.bitcast(thr, jnp.int32)
        tw = tb ^ ((tb >> 31) | int_min)
        cand_ref[...] = jnp.full(cand_ref.shape, -1, jnp.int32)

        def tied_body(b, carry):
            flip = jnp.where(((tw >> b) & 1) == 1, 0, -1)
            for c in range(nc_max):
                cand_ref[c] = cand_ref[c] & (planes_ref[b, c] ^ flip)
            return carry

        lax.fori_loop(0, 32, tied_body, 0)

        sub = lax.broadcasted_iota(jnp.int32, (SUBLANES, lw2), 0)
        passes = [("chunk", k) for k in reversed(range(max(1, (nc_max - 1).bit_length())))]
        passes += [("group", k) for k in reversed(range(5))] + [("sublane", k) for k in reversed(range(3))]
        left, cut = need, jnp.zeros((1, lw2), jnp.int32)
        for kind, k in passes:
            def high(c, kind=kind, k=k):
                if kind == "chunk":
                    return jnp.int32(-1 if (c >> k) & 1 else 0)
                if kind == "group":
                    word = sum(1 << i for i in range(32) if ((31 - i) >> k) & 1)
                    return jnp.int32(word - (1 << 32) if word >= (1 << 31) else word)
                return jnp.where(((sub >> k) & 1) == 1, -1, 0)
            cnt = jnp.zeros((SUBLANES, lw2), jnp.int32)
            for c in range(nc_max):
                cnt = cnt + lax.population_count(cand_ref[c] & ~high(c))
            c0 = cnt.astype(f32).sum(axis=0, keepdims=True)
            low = c0 >= left
            left = jnp.where(low, left, left - c0)
            weight = {"chunk": 8, "group": 3, "sublane": 0}[kind] + k
            cut = cut | jnp.where(low, 0, 1 << weight)
            flip = jnp.where(low, -1, 0)
            for c in range(nc_max):
                cand_ref[c] = cand_ref[c] & (high(c) ^ flip)
        cut = jnp.where(ties, cut, jnp.int32(2 ** 30))
        write_bias(lambda blk, row0: jnp.logical_or(
            blk > thr, jnp.logical_and(blk == thr, rows(row0) <= cut)))

    m_ref[...] = jnp.full(m_ref.shape, NEG_BIG, f32)
    acc_ref[...] = jnp.zeros(acc_ref.shape, f32)

    n_steps = n_chunks * (ts // ta)
    last = n_steps - 1

    def logits(step, buf):
        row0 = pl.multiple_of(step * ta, ta)
        kc = key_rows(kt_ref, row0, ta)
        for hv in range(nvh):
            bias = sc_ref[pl.ds(row0, ta), :]
            if lw2 < lw:
                bias = jnp.concatenate([bias] * (lw // lw2), axis=1)
            s = lax.dot_general(kc, q_ref[hv], _NT, preferred_element_type=f32) + bias
            s_ref[buf, hv] = s
            mx_ref[buf, hv] = s.reshape(ta // SUBLANES, SUBLANES, lw).max(axis=0).max(axis=0, keepdims=True)

    def absorb(step, buf):
        row0 = pl.multiple_of(step * ta, ta)
        for hv in range(nvh):
            g = (hv * hp) // GROUP
            m_old = m_ref[hv]
            m_new = jnp.maximum(m_old, mx_ref[buf, hv])
            alpha = jnp.exp2(m_old - m_new)
            p = jnp.exp2(s_ref[buf, hv] - m_new).astype(bf16)
            vt = jnp.concatenate(
                [vt_ref[0, g * HEAD_DIM:(g + 1) * HEAD_DIM, pl.ds(row0, ta)].astype(bf16), ones_rows], axis=0)
            acc_ref[hv] = alpha * acc_ref[hv] + jnp.dot(vt, p, preferred_element_type=f32)
            m_ref[hv] = m_new

    logits(0, 0)

    def att_body(i, carry):
        logits(jnp.minimum(2 * i + 1, last), 1)
        absorb(2 * i, 0)
        logits(jnp.minimum(2 * i + 2, last), 0)
        absorb(2 * i + 1, 1)
        return carry

    lax.fori_loop(0, n_steps // 2, att_body, 0)

    @pl.when(n_steps % 2 == 1)
    def _():
        absorb(last, 0)

    for hv in range(nvh):
        o = acc_ref[hv]
        ot_ref[hv * HEAD_DIM:(hv + 1) * HEAD_DIM, :] = o[:HEAD_DIM] / o[HEAD_DIM:HEAD_DIM + 1]
    if hp == 1:
        o_ref[...] = ot_ref[...].T.astype(o_ref.dtype)
    else:
        o_ref[0] = ot_ref[...].astype(o_ref.dtype)


def _attention(qi_hm, wi2d, q_hm, kwt_all, kt_all, vt_all, *, B, T, past, topk):
    tq = min(256, T)
    ts = 256
    ta = 256
    hp = ts // tq
    assert past % ts == 0 and (tq == ts or tq == T) and tq % CHUNK == 0 and hp in (1, 2, 4)
    nvh = N_HEADS // hp
    lw = hp * tq
    lw2 = max(tq, LANES)
    nq = T // tq
    nb = B * nq
    s_pad = (past + T + ts - 1) // ts * ts
    assert kwt_all.shape == kt_all.shape == vt_all.shape == (B, LANES, s_pad)
    N = B * T

    def pack(a):
        if hp == 1:
            return a
        a = a.reshape(nvh, hp, nb, tq, LANES)
        return jnp.transpose(a, (0, 2, 1, 3, 4)).reshape(nvh, nb * lw, LANES)

    wi3 = jnp.transpose(wi2d.reshape(nb, tq, nvh, hp), (0, 2, 3, 1)).reshape(nb, nvh, lw)
    hm_spec = pl.BlockSpec((nvh, lw, LANES), lambda b, j: (0, b * nq + j, 0))
    if hp == 1:
        out_spec = pl.BlockSpec((tq, ATTN_WIDTH), lambda b, j: (b * nq + j, 0))
        out_shape = jax.ShapeDtypeStruct((N, ATTN_WIDTH), bf16)
    else:
        out_spec = pl.BlockSpec((1, nvh * HEAD_DIM, lw), lambda b, j: (b * nq + j, 0, 0))
        out_shape = jax.ShapeDtypeStruct((nb, nvh * HEAD_DIM, lw), bf16)
    out = pl.pallas_call(
        functools.partial(_attn_kernel, tq=tq, hp=hp, ts=ts, t1=128, ta=ta, past=past, topk=topk),
        grid=(B, nq),
        in_specs=[
            hm_spec,
            pl.BlockSpec((1, nvh, lw), lambda b, j: (b * nq + j, 0, 0)),
            hm_spec,
        ] + [pl.BlockSpec((1, LANES, s_pad), lambda b, j: (b, 0, 0))] * 3,
        out_specs=out_spec,
        out_shape=out_shape,
        scratch_shapes=[pltpu.VMEM((s_pad, lw2), f32), pltpu.VMEM((nvh, 1, lw), f32),
                        pltpu.VMEM((nvh, V_ROWS, lw), f32), pltpu.VMEM((nvh * HEAD_DIM, lw), f32),
                        pltpu.VMEM((2, nvh, ta, lw), f32), pltpu.VMEM((2, nvh, 1, lw), f32),
                        pltpu.VMEM((SUBLANES, lw2), f32),
                        pltpu.VMEM((32, s_pad // ts, SUBLANES, lw2), jnp.int32),
                        pltpu.VMEM((s_pad // ts, SUBLANES, lw2), jnp.int32)],
        compiler_params=pltpu.CompilerParams(dimension_semantics=("arbitrary", "arbitrary"),
                                             vmem_limit_bytes=VMEM_LIMIT),
        name="dsa_attention",
    )(pack(qi_hm), wi3, pack(q_hm), kwt_all, kt_all, vt_all)
    if hp > 1:
        out = out.reshape(nb, nvh, HEAD_DIM, hp, tq)
        out = jnp.transpose(out, (0, 4, 1, 3, 2)).reshape(N, ATTN_WIDTH)
    return out


FF_CHUNK = 256


def _post_kernel(x_ref, c_ref, a_ref, wo_ref, g1_ref, b1_ref, wgu_ref, wd_ref, g2_ref, b2_ref, o_ref, acc_ref):
    w16 = lambda r, rows, cols: r[0, rows, cols].astype(bf16)
    full = slice(None)
    mix = (jnp.dot(c_ref[...], w16(wo_ref, slice(0, C_CONV), full), preferred_element_type=f32)
           + jnp.dot(a_ref[...], w16(wo_ref, slice(C_CONV, D_MODEL), full), preferred_element_type=f32))
    x1 = _layer_norm(ALPHA * x_ref[...] + mix, g1_ref[...], b1_ref[...])
    x1b = x1.astype(bf16)
    for c in range(D_FF // FF_CHUNK):
        lo = c * FF_CHUNK
        gate = jnp.dot(x1b, w16(wgu_ref, full, slice(lo, lo + FF_CHUNK)), preferred_element_type=f32)
        up = jnp.dot(x1b, w16(wgu_ref, full, slice(D_FF + lo, D_FF + lo + FF_CHUNK)), preferred_element_type=f32)
        act = (gate * jax.nn.sigmoid(gate) * up).astype(bf16)
        part = jnp.dot(act, w16(wd_ref, slice(lo, lo + FF_CHUNK), full), preferred_element_type=f32)
        if c == 0:
            acc_ref[...] = part
        else:
            acc_ref[...] += part
    o_ref[...] = _layer_norm(ALPHA * x1 + acc_ref[...], g2_ref[...], b2_ref[...])


def _post(x2d, conv2d, attn2d, layer, w_o, g1, b1, w_gate_up, w_down, g2, b2):
    N = x2d.shape[0]
    tm = min(512, N)
    row = lambda w: pl.BlockSpec((tm, w), lambda i: (i, 0))
    vec_spec = pl.BlockSpec((1, D_MODEL), lambda i: (0, 0))
    weight = lambda a: pl.BlockSpec((1,) + a.shape[1:], lambda i: (layer, 0, 0), pipeline_mode=pl.Buffered(1))
    vec = lambda a: a.reshape(1, D_MODEL)
    return pl.pallas_call(
        _post_kernel,
        grid=(N // tm,),
        in_specs=[row(D_MODEL), row(C_CONV), row(ATTN_WIDTH), weight(w_o), vec_spec, vec_spec,
                  weight(w_gate_up), weight(w_down), vec_spec, vec_spec],
        out_specs=row(D_MODEL),
        out_shape=jax.ShapeDtypeStruct((N, D_MODEL), f32),
        scratch_shapes=[pltpu.VMEM((tm, D_MODEL), f32)],
        compiler_params=pltpu.CompilerParams(dimension_semantics=("arbitrary",), vmem_limit_bytes=VMEM_LIMIT),
        name="post",
    )(x2d, conv2d, attn2d, w_o, vec(g1), vec(b1), w_gate_up, w_down, vec(g2), vec(b2))


def _layer(x, conv_state, k_past, v_past, ki_past, wts):
    (layer, w_in, w_tail, conv_w, conv_b, cln_g, cln_b, w_o, ln1_g, ln1_b, w_gate_up, w_down, ln2_g, ln2_b) = wts
    B, T, _ = x.shape
    past = k_past.shape[1]
    N = B * T
    x2d = x.reshape(N, D_MODEL)
    u, q_hm, qi_hm, kt, vt, kwt = _inproj(x2d, layer, w_in, w_tail, T, past)

    u3 = u.reshape(B, T, C_CONV)
    conv_out = _conv(u3, conv_state, conv_w, conv_b, cln_g, cln_b)
    new_conv = jnp.concatenate([conv_state.astype(f32), u3], axis=1)[:, -(CONV_WIDTH - 1):]

    per_head = lambda a: jnp.transpose(a.reshape(B, N_KV_HEADS, HEAD_DIM, T), (0, 3, 1, 2))
    k, v = per_head(kt), per_head(vt)
    ki = jnp.transpose(kwt[:, :IDX_DIM, :], (0, 2, 1))
    wi2d = jnp.transpose(kwt[:, IDX_DIM:IDX_DIM + N_IDX_HEADS, :], (0, 2, 1)).reshape(N, N_IDX_HEADS)
    S = past + T
    s_pad = -(-S // 256) * 256
    if past == 0 and s_pad == T:
        kwt_all, kt_all, vt_all = kwt, kt, vt
    else:
        def with_cache(old, new):
            old = jnp.moveaxis(old.reshape(B, past, -1), 1, 2)
            if old.shape[1] < LANES:
                old = jnp.concatenate([old, jnp.zeros((B, LANES - old.shape[1], past), f32)], axis=1)
            return jnp.concatenate([old, new, jnp.zeros((B, LANES, s_pad - S), f32)], axis=2)

        kwt_all, kt_all, vt_all = with_cache(ki_past, kwt), with_cache(k_past, kt), with_cache(v_past, vt)
    attn = _attention(qi_hm, wi2d, q_hm, kwt_all, kt_all, vt_all, B=B, T=T, past=past, topk=min(TOPK_MAX, S // 4))

    y = _post(x2d, conv_out.reshape(N, C_CONV), attn, layer, w_o, ln1_g, ln1_b, w_gate_up, w_down, ln2_g, ln2_b)
    return y.reshape(B, T, D_MODEL), k, v, ki, new_conv


def kernel(x_prompt, x_sample, cache_k, cache_v, cache_k_idx, state_conv, w_in, conv_w, conv_b, conv_ln_g,
           conv_ln_b, w_o, ln1_g, ln1_b, w_gate_up, w_down, ln2_g, ln2_b):
    Bp = x_prompt.shape[0]
    dt = x_prompt.dtype
    empty_kv = jnp.zeros((Bp, 0, N_KV_HEADS, HEAD_DIM), dt)
    empty_ki = jnp.zeros((Bp, 0, IDX_DIM), dt)
    zero_conv = jnp.zeros((Bp, CONV_WIDTH - 1, C_CONV), dt)
    hp, hs = x_prompt, x_sample
    outs_p, outs_s = [], []
    for l in range(DEPTH):
        wts = (
            l, w_in, jnp.pad(w_in[l, :, _O_KW:].astype(bf16), ((0, 0), (0, _O_KW + LANES - IN_DIM))),
            conv_w[l], conv_b[l], conv_ln_g[l], conv_ln_b[l],
            w_o, ln1_g[l], ln1_b[l], w_gate_up, w_down, ln2_g[l], ln2_b[l],
        )
        hp, *op = _layer(hp, zero_conv, empty_kv, empty_kv, empty_ki, wts)
        hs, *os_ = _layer(hs, state_conv[l], cache_k[l], cache_v[l], cache_k_idx[l], wts)
        outs_p.append(op)
        outs_s.append(os_)
    stack = lambda outs, i: jnp.stack([o[i] for o in outs])
    return (hp, hs,
            stack(outs_p, 0), stack(outs_p, 1), stack(outs_p, 2), stack(outs_p, 3),
            stack(outs_s, 0), stack(outs_s, 1), stack(outs_s, 2), stack(outs_s, 3))
```

```python
import functools
import math

import jax
import jax.numpy as jnp
from jax import lax
from jax.experimental import pallas as pl
from jax.experimental.pallas import tpu as pltpu

D_MODEL = 1024
CHUNK = 64
C_CONV = D_MODEL // 2
CONV_WIDTH = 31
HEAD_DIM = 64
N_HEADS = (D_MODEL // 2) // HEAD_DIM
N_KV_HEADS = 2
GROUP = N_HEADS // N_KV_HEADS
ATTN_WIDTH = N_HEADS * HEAD_DIM
N_IDX_HEADS = 8
IDX_DIM = 64
TOPK_MAX = 256
ROPE_THETA = 10000.0
D_FF = -(-8 * D_MODEL // (3 * 256)) * 256
IN_DIM = 2 * C_CONV + (N_HEADS + 2 * N_KV_HEADS) * HEAD_DIM + N_IDX_HEADS * IDX_DIM + IDX_DIM + N_IDX_HEADS
DEPTH = 2
ALPHA = (2 * DEPTH) ** 0.25
ATTN_SCALE = HEAD_DIM ** -0.5
IDX_SCALE = IDX_DIM ** -0.5
IDX_HEAD_SCALE = N_IDX_HEADS ** -0.5
LN_EPS = 1e-5
Q_SCALE = ATTN_SCALE * math.log2(math.e)

LANES = 128
SUBLANES = 8
VMEM_LIMIT = 56 * 1024 * 1024
IN_PAD = -(-IN_DIM // LANES) * LANES
HALO = 32
NEG_BIG = -1e30
INT_MIN = -(2 ** 31)
F32_LOWEST = float(jnp.finfo(jnp.float32).min)
NEG_INF_WORD = 0x007FFFFF
V_ROWS = HEAD_DIM + 16

_O_A = 0
_O_G = _O_A + C_CONV
_O_Q = _O_G + C_CONV
_O_K = _O_Q + ATTN_WIDTH
_O_V = _O_K + N_KV_HEADS * HEAD_DIM
_O_QI = _O_V + N_KV_HEADS * HEAD_DIM
_O_KW = _O_QI + N_IDX_HEADS * IDX_DIM

_NT = (((1,), (1,)), ((), ()))

f32 = jnp.float32
bf16 = jnp.bfloat16


def _layer_norm(y, g, b):
    mu = jnp.mean(y, axis=-1, keepdims=True)
    d = y - mu
    var = jnp.mean(d * d, axis=-1, keepdims=True)
    return d * lax.rsqrt(var + LN_EPS) * g + b


def _rope128(x, cos, sa, sb):
    return x * cos + pltpu.roll(x, 96, 1) * sa + pltpu.roll(x, 32, 1) * sb


def _inproj_kernel(x_ref, w_ref, wt_ref, cos_ref, sa_ref, sb_ref, cosk_ref, sak_ref, sbk_ref,
                   u_ref, q_ref, qi_ref, kt_ref, vt_ref, kwt_ref, *, seq):
    xb = x_ref[...].astype(bf16)
    tm = xb.shape[0]

    def put_t(ref, a):
        at = a.T
        for s in range(ref.shape[0]):
            ref[s] = at[:, s * seq:(s + 1) * seq] if tm > seq else at

    def mm(lo, width):
        return jnp.dot(xb, w_ref[0, :, lo:lo + width].astype(bf16), preferred_element_type=f32)

    u_ref[...] = mm(_O_A, C_CONV) * jax.nn.sigmoid(mm(_O_G, C_CONV))

    cos, sa, sb = cos_ref[...], sa_ref[...], sb_ref[...]
    low = lax.broadcasted_iota(jnp.int32, (tm, LANES), 1) < HEAD_DIM

    def place(r, src, dst):
        if src != dst:
            r = pltpu.roll(r, HEAD_DIM, 1)
        return jnp.where(low if dst == 0 else jnp.logical_not(low), r, 0.0).astype(bf16)

    q = mm(_O_Q, ATTN_WIDTH)
    qi = mm(_O_QI, N_IDX_HEADS * IDX_DIM)
    for c in range(ATTN_WIDTH // LANES):
        r = _rope128(q[:, c * LANES:(c + 1) * LANES], cos, sa, sb) * Q_SCALE
        for half in range(2):
            h = 2 * c + half
            q_ref[h] = place(r, half, h // GROUP)
        r = _rope128(qi[:, c * LANES:(c + 1) * LANES], cos, sa, sb)
        for half in range(2):
            qi_ref[2 * c + half] = place(r, half, 0)
    put_t(kt_ref, _rope128(mm(_O_K, N_KV_HEADS * HEAD_DIM), cos, sa, sb))
    put_t(vt_ref, mm(_O_V, N_KV_HEADS * HEAD_DIM))
    kw = jnp.dot(xb, wt_ref[...], preferred_element_type=f32)
    put_t(kwt_ref, _rope128(kw, cosk_ref[...], sak_ref[...], sbk_ref[...]))


def _rope_tables(T, past, rows):
    half = HEAD_DIM // 2
    inv = ROPE_THETA ** (-jnp.arange(half, dtype=f32) / half)
    pos = (jnp.arange(T, dtype=jnp.int32) + past).astype(f32)
    ang = pos[:, None] * inv[None, :]
    cos, sin = jnp.cos(ang), jnp.sin(ang)
    zero = jnp.zeros_like(sin)
    cos64 = jnp.concatenate([cos, cos], axis=1)
    sa64 = jnp.concatenate([-sin, zero], axis=1)
    sb64 = jnp.concatenate([zero, sin], axis=1)
    two = lambda a: jnp.concatenate([a, a], axis=1)
    wscale = jnp.full((T, N_IDX_HEADS), IDX_HEAD_SCALE * IDX_SCALE, f32)
    ones = jnp.ones((T, LANES - IDX_DIM - N_IDX_HEADS), f32)
    cosk = jnp.concatenate([cos64, wscale, ones], axis=1)
    zpad = jnp.zeros((T, LANES - IDX_DIM), f32)
    tabs = (two(cos64), two(sa64), two(sb64), cosk,
            jnp.concatenate([sa64, zpad], axis=1), jnp.concatenate([sb64, zpad], axis=1))
    reps = rows // T
    return tuple(jnp.tile(t, (reps, 1)) for t in tabs)


def _inproj(x2d, layer, w_in, w_tail, T, past):
    N = x2d.shape[0]
    tm = min(512, N)
    rows = max(T, tm)
    tabs = _rope_tables(T, past, rows)
    nt = rows // tm
    row_spec = lambda w: pl.BlockSpec((tm, w), lambda i: (i, 0))
    tab_spec = pl.BlockSpec((tm, LANES), lambda i: (i % nt, 0))
    hm_spec = pl.BlockSpec((N_HEADS, tm, LANES), lambda i: (0, i, 0))
    if tm <= T:
        per = T // tm
        t_spec = pl.BlockSpec((1, LANES, tm), lambda i: (i // per, 0, i % per))
    else:
        t_spec = pl.BlockSpec((tm // T, LANES, T), lambda i: (i, 0, 0))
    t_shape = jax.ShapeDtypeStruct((N // T, LANES, T), f32)
    return pl.pallas_call(
        functools.partial(_inproj_kernel, seq=T),
        grid=(N // tm,),
        in_specs=[row_spec(D_MODEL),
                  pl.BlockSpec((1, D_MODEL, IN_DIM), lambda i: (layer, 0, 0), pipeline_mode=pl.Buffered(1)),
                  pl.BlockSpec((D_MODEL, LANES), lambda i: (0, 0))] + [tab_spec] * 6,
        out_specs=[row_spec(C_CONV), hm_spec, hm_spec, t_spec, t_spec, t_spec],
        out_shape=[
            jax.ShapeDtypeStruct((N, C_CONV), f32),
            jax.ShapeDtypeStruct((N_HEADS, N, LANES), bf16),
            jax.ShapeDtypeStruct((N_IDX_HEADS, N, LANES), bf16),
            t_shape, t_shape, t_shape,
        ],
        compiler_params=pltpu.CompilerParams(dimension_semantics=("arbitrary",), vmem_limit_bytes=VMEM_LIMIT),
        name="inproj",
    )(x2d, w_in, w_tail, *tabs)


def _conv_kernel(u_ref, halo_ref, st_ref, w_ref, b_ref, g_ref, bt_ref, o_ref, f_ref, sh_ref, *, tc, rb):
    i = pl.program_id(1)
    f_ref[0:HALO, :] = jnp.where(i == 0, st_ref[0], halo_ref[0])
    f_ref[HALO:HALO + tc, :] = u_ref[0]
    span = tc + HALO - SUBLANES
    for c in range(1, SUBLANES):
        sh_ref[c, 0:span, :] = f_ref[c:c + span, :]
    lead = HALO - (CONV_WIDTH - 1)
    for r in range(tc // rb):
        acc = jnp.zeros((rb, C_CONV), f32)
        for j in range(CONV_WIDTH):
            c = (lead + j) % SUBLANES
            lo = r * rb + lead + j - c
            rows = f_ref[lo:lo + rb, :] if c == 0 else sh_ref[c, lo:lo + rb, :]
            acc = acc + rows * jnp.tile(w_ref[j], (rb // SUBLANES, 1))
        y = _layer_norm(acc + b_ref[...], g_ref[...], bt_ref[...])
        o_ref[0, r * rb:(r + 1) * rb, :] = (y * jax.nn.sigmoid(y)).astype(o_ref.dtype)


def _conv(u3, state, conv_w, conv_b, ln_g, ln_b):
    B, T, _ = u3.shape
    tc = min(256, T)
    st = jnp.pad(state.astype(f32), ((0, 0), (HALO - (CONV_WIDTH - 1), 0), (0, 0)))
    wp = jnp.broadcast_to(conv_w[:, None, :], (CONV_WIDTH, SUBLANES, C_CONV))
    vec = lambda a: a.reshape(1, C_CONV)
    hb = tc // HALO
    return pl.pallas_call(
        functools.partial(_conv_kernel, tc=tc, rb=32),
        grid=(B, T // tc),
        in_specs=[
            pl.BlockSpec((1, tc, C_CONV), lambda b, i: (b, i, 0)),
            pl.BlockSpec((1, HALO, C_CONV), lambda b, i: (b, jnp.maximum(i * hb - 1, 0), 0)),
            pl.BlockSpec((1, HALO, C_CONV), lambda b, i: (b, 0, 0)),
            pl.BlockSpec((CONV_WIDTH, SUBLANES, C_CONV), lambda b, i: (0, 0, 0)),
        ] + [pl.BlockSpec((1, C_CONV), lambda b, i: (0, 0))] * 3,
        out_specs=pl.BlockSpec((1, tc, C_CONV), lambda b, i: (b, i, 0)),
        out_shape=jax.ShapeDtypeStruct((B, T, C_CONV), bf16),
        scratch_shapes=[pltpu.VMEM((HALO + tc, C_CONV), f32),
                        pltpu.VMEM((SUBLANES, HALO + tc - SUBLANES, C_CONV), f32)],
        compiler_params=pltpu.CompilerParams(dimension_semantics=("arbitrary", "arbitrary"),
                                             vmem_limit_bytes=VMEM_LIMIT),
        name="conv",
    )(u3, u3, st, wp, vec(conv_b), vec(ln_g), vec(ln_b))


def _key_to_f32(key):
    bits = key ^ ((key >> 31) & jnp.int32(0x7FFFFFFF))
    return pltpu.bitcast(bits, f32)


def _attn_kernel(qi_ref, wi_ref, q_ref, kwt_ref, kt_ref, vt_ref, o_ref, sc_ref, m_ref, acc_ref, ot_ref, s_ref,
                 mx_ref, st_ref, planes_ref, cand_ref,
                 *, tq, hp, ts, t1, ta, past, topk):
    nvh = N_HEADS // hp
    lw = hp * tq
    lw2 = sc_ref.shape[1]
    j = pl.program_id(1)
    n_full = (past + j * tq) // ts
    n_chunks = n_full + 1
    kf = float(topk)

    def key_rows(ref, row0, n):
        x = ref[0, :, pl.ds(row0, n)].astype(f32)
        if x.shape[0] < LANES:
            x = jnp.concatenate([x, jnp.zeros((LANES - x.shape[0], n), f32)], axis=0)
        return x.T.astype(bf16)

    ones_rows = jnp.where(lax.broadcasted_iota(jnp.int32, (V_ROWS - HEAD_DIM, ta), 0) == 0, 1.0, 0.0).astype(bf16)

    q_chunk = (lax.broadcasted_iota(jnp.int32, (t1, lw2), 1) % tq) // CHUNK

    nc_max = cand_ref.shape[0]
    int_min = jnp.int32(INT_MIN)
    cand_ref[...] = jnp.zeros(cand_ref.shape, jnp.int32)

    def write_planes(c):
        row0 = pl.multiple_of(c * ts, ts)
        for lt in range(lw2 // LANES):
            lanes = slice(lt * LANES, (lt + 1) * LANES)
            bits = pltpu.bitcast(sc_ref[pl.ds(row0, ts), lanes], jnp.int32)
            u = bits ^ ((bits >> 31) | int_min)
            w = [u[i * SUBLANES:(i + 1) * SUBLANES, :] for i in range(32)]
            j, m = 16, 0x0000FFFF
            while j:
                k = 0
                while k < 32:
                    t = (w[k] ^ (w[k + j] >> j)) & jnp.int32(m)
                    w[k] = w[k] ^ t
                    w[k + j] = w[k + j] ^ (t << j)
                    k = (k + j + 1) & ~j
                j >>= 1
                m = m ^ (m << j)
            gone = jnp.full((SUBLANES, LANES), -1, jnp.int32)
            for b in range(32):
                plane = w[31 - b]
                planes_ref[b, c, :, lanes] = plane
                gone = gone & (plane if (NEG_INF_WORD >> b) & 1 else ~plane)
            cand_ref[c, :, lanes] = ~gone

    def score_chunk(c, masked):
        for r in range(ts // t1):
            row0 = pl.multiple_of(c * ts + r * t1, t1)
            kic = key_rows(kwt_ref, row0, t1)
            acc = jnp.zeros((t1, lw), f32)
            for hv in range(nvh):
                d = lax.dot_general(kic, qi_ref[hv], _NT, preferred_element_type=f32)
                acc = acc + jnp.maximum(d, 0.0) * wi_ref[0, hv:hv + 1, :]
            width = lw
            while width > tq:
                width //= 2
                acc = acc + pltpu.roll(acc, width, 1)
            acc = acc[:, :lw2]
            if masked:
                k_chunk = (lax.broadcasted_iota(jnp.int32, (t1, lw2), 0) + r * t1) // CHUNK
                acc = jnp.where(k_chunk <= q_chunk, acc, -jnp.inf)
            sc_ref[pl.ds(row0, t1), :] = acc
        write_planes(c)

    def pair_body(i, carry):
        score_chunk(2 * i, False)
        score_chunk(2 * i + 1, False)
        return carry

    lax.fori_loop(0, n_full // 2, pair_body, 0)

    @pl.when(n_full % 2 == 1)
    def _():
        score_chunk(n_full - 1, False)

    score_chunk(n_full, True)

    def count(preds):
        def body(c, accs):
            row0 = pl.multiple_of(c * ts, ts)
            blk = sc_ref[pl.ds(row0, ts), :]
            out = []
            for pred, acc in zip(preds, accs):
                m = jnp.where(pred(blk, row0), 1.0, 0.0)
                out.append(acc + m.reshape(ts // SUBLANES, SUBLANES, lw2).sum(axis=0))
            return tuple(out)
        init = tuple(jnp.zeros((SUBLANES, lw2), f32) for _ in preds)
        accs = lax.fori_loop(0, n_chunks, body, init)
        return [a.sum(axis=0, keepdims=True) for a in accs]

    def cut_counts(thr):
        c_ge, c_gt = count([lambda blk, row0: blk >= thr, lambda blk, row0: blk > thr])
        st_ref[0:1, :] = thr
        st_ref[1:2, :] = c_ge
        st_ref[2:3, :] = c_gt

    def idle_body(c, carry):
        planes_ref[:, c] = jnp.zeros((32, SUBLANES, lw2), jnp.int32)
        return carry

    lax.fori_loop(n_chunks, nc_max, idle_body, 0)

    def bit_count(b, flip):
        cnt = jnp.zeros((SUBLANES, lw2), jnp.int32)
        tot = jnp.zeros((SUBLANES, lw2), jnp.int32)
        for c in range(nc_max):
            cand = cand_ref[c]
            if flip is None:
                tot = tot + lax.population_count(cand)
            else:
                cand = cand & (planes_ref[b + 1, c] ^ flip)
                cand_ref[c] = cand
            cnt = cnt + lax.population_count(cand & planes_ref[b, c])
        red = lambda a: a.astype(f32).sum(axis=0, keepdims=True)
        return red(cnt), red(tot)

    def decide(c1, left, word, b):
        take = c1 >= left
        return (jnp.where(take, left, left - c1), word | jnp.where(take, lax.shift_left(jnp.int32(1), b), 0),
                jnp.where(take, 0, -1))

    c1, n_adm = bit_count(31, None)
    state = decide(c1, jnp.full((1, lw2), kf, f32), jnp.zeros((1, lw2), jnp.int32), 31)

    def radix_body(it, state):
        left, word, flip = state
        b = 30 - it
        c1, _ = bit_count(b, flip)
        return decide(c1, left, word, b)

    _, word, _ = lax.fori_loop(0, 31, radix_body, state)
    few = n_adm < kf
    cut_counts(jnp.where(few, F32_LOWEST, _key_to_f32(word ^ int_min)))

    good = jnp.logical_or(few, jnp.logical_and(st_ref[2:3, :] < kf, st_ref[1:2, :] >= kf))

    @pl.when(jnp.max(jnp.where(good, 0.0, 1.0)) > 0.0)
    def _():
        def bisect(it, key):
            cand = key + lax.shift_left(jnp.int32(1), 31 - it)
            thr = _key_to_f32(cand)
            cnt, = count([lambda blk, row0: blk >= thr])
            return jnp.where(cnt >= kf, cand, key)

        key = lax.fori_loop(0, 32, bisect, jnp.full((1, lw2), INT_MIN, jnp.int32))
        cut_counts(jnp.where(key == INT_MIN, F32_LOWEST, _key_to_f32(key)))

    thr, c_ge, c_gt = st_ref[0:1, :], st_ref[1:2, :], st_ref[2:3, :]
    ties = c_ge > kf
    need = kf - c_gt
    any_ties = jnp.max(jnp.where(ties, 1.0, 0.0))

    def write_bias(select):
        def body(c, carry):
            row0 = pl.multiple_of(c * ts, ts)
            blk = sc_ref[pl.ds(row0, ts), :]
            sc_ref[pl.ds(row0, ts), :] = jnp.where(select(blk, row0), 0.0, NEG_BIG)
            return carry
        lax.fori_loop(0, n_chunks, body, 0)

    @pl.when(any_ties == 0.0)
    def _():
        write_bias(lambda blk, row0: blk >= thr)

    @pl.when(any_ties > 0.0)
    def _():
        def rows(row0):
            return lax.broadcasted_iota(jnp.int32, (ts, lw2), 0) + row0

        tb = pltpu.bitcast(thr, jnp.int32)
        tw = tb ^ ((tb >> 31) | int_min)
        cand_ref[...] = jnp.full(cand_ref.shape, -1, jnp.int32)

        def tied_body(b, carry):
            flip = jnp.where(((tw >> b) & 1) == 1, 0, -1)
            for c in range(nc_max):
                cand_ref[c] = cand_ref[c] & (planes_ref[b, c] ^ flip)
            return carry

        lax.fori_loop(0, 32, tied_body, 0)

        sub = lax.broadcasted_iota(jnp.int32, (SUBLANES, lw2), 0)
        passes = [("chunk", k) for k in reversed(range(max(1, (nc_max - 1).bit_length())))]
        passes += [("group", k) for k in reversed(range(5))] + [("sublane", k) for k in reversed(range(3))]
        left, cut = need, jnp.zeros((1, lw2), jnp.int32)
        for kind, k in passes:
            def high(c, kind=kind, k=k):
                if kind == "chunk":
                    return jnp.int32(-1 if (c >> k) & 1 else 0)
                if kind == "group":
                    word = sum(1 << i for i in range(32) if ((31 - i) >> k) & 1)
                    return jnp.int32(word - (1 << 32) if word >= (1 << 31) else word)
                return jnp.where(((sub >> k) & 1) == 1, -1, 0)
            cnt = jnp.zeros((SUBLANES, lw2), jnp.int32)
            for c in range(nc_max):
                cnt = cnt + lax.population_count(cand_ref[c] & ~high(c))
            c0 = cnt.astype(f32).sum(axis=0, keepdims=True)
            low = c0 >= left
            left = jnp.where(low, left, left - c0)
            weight = {"chunk": 8, "group": 3, "sublane": 0}[kind] + k
            cut = cut | jnp.where(low, 0, 1 << weight)
            flip = jnp.where(low, -1, 0)
            for c in range(nc_max):
                cand_ref[c] = cand_ref[c] & (high(c) ^ flip)
        cut = jnp.where(ties, cut, jnp.int32(2 ** 30))
        write_bias(lambda blk, row0: jnp.logical_or(
            blk > thr, jnp.logical_and(blk == thr, rows(row0) <= cut)))

    m_ref[...] = jnp.full(m_ref.shape, NEG_BIG, f32)
    acc_ref[...] = jnp.zeros(acc_ref.shape, f32)

    n_steps = n_chunks * (ts // ta)
    last = n_steps - 1

    def logits(step, buf):
        row0 = pl.multiple_of(step * ta, ta)
        kc = key_rows(kt_ref, row0, ta)
        for hv in range(nvh):
            bias = sc_ref[pl.ds(row0, ta), :]
            if lw2 < lw:
                bias = jnp.concatenate([bias] * (lw // lw2), axis=1)
            s = lax.dot_general(kc, q_ref[hv], _NT, preferred_element_type=f32) + bias
            s_ref[buf, hv] = s
            mx_ref[buf, hv] = s.reshape(ta // SUBLANES, SUBLANES, lw).max(axis=0).max(axis=0, keepdims=True)

    def absorb(step, buf):
        row0 = pl.multiple_of(step * ta, ta)
        for hv in range(nvh):
            g = (hv * hp) // GROUP
            m_old = m_ref[hv]
            m_new = jnp.maximum(m_old, mx_ref[buf, hv])
            alpha = jnp.exp2(m_old - m_new)
            p = jnp.exp2(s_ref[buf, hv] - m_new).astype(bf16)
            vt = jnp.concatenate(
                [vt_ref[0, g * HEAD_DIM:(g + 1) * HEAD_DIM, pl.ds(row0, ta)].astype(bf16), ones_rows], axis=0)
            acc_ref[hv] = alpha * acc_ref[hv] + jnp.dot(vt, p, preferred_element_type=f32)
            m_ref[hv] = m_new

    logits(0, 0)

    def att_body(i, carry):
        logits(jnp.minimum(2 * i + 1, last), 1)
        absorb(2 * i, 0)
        logits(jnp.minimum(2 * i + 2, last), 0)
        absorb(2 * i + 1, 1)
        return carry

    lax.fori_loop(0, n_steps // 2, att_body, 0)

    @pl.when(n_steps % 2 == 1)
    def _():
        absorb(last, 0)

    for hv in range(nvh):
        o = acc_ref[hv]
        ot_ref[hv * HEAD_DIM:(hv + 1) * HEAD_DIM, :] = o[:HEAD_DIM] / o[HEAD_DIM:HEAD_DIM + 1]
    if hp == 1:
        o_ref[...] = ot_ref[...].T.astype(o_ref.dtype)
    else:
        o_ref[0] = ot_ref[...].astype(o_ref.dtype)


def _attention(qi_hm, wi2d, q_hm, kwt_all, kt_all, vt_all, *, B, T, past, topk):
    tq = min(256, T)
    ts = 256
    ta = 256
    hp = ts // tq
    assert past % ts == 0 and (tq == ts or tq == T) and tq % CHUNK == 0 and hp in (1, 2, 4)
    nvh = N_HEADS // hp
    lw = hp * tq
    lw2 = max(tq, LANES)
    nq = T // tq
    nb = B * nq
    s_pad = (past + T + ts - 1) // ts * ts
    assert kt_all.shape == vt_all.shape == (B, LANES, s_pad)
    assert kwt_all.shape in ((B, LANES, s_pad), (B, IDX_DIM, s_pad))
    N = B * T

    def pack(a):
        if hp == 1:
            return a
        a = a.reshape(nvh, hp, nb, tq, LANES)
        return jnp.transpose(a, (0, 2, 1, 3, 4)).reshape(nvh, nb * lw, LANES)

    wi3 = jnp.transpose(wi2d.reshape(nb, tq, nvh, hp), (0, 2, 3, 1)).reshape(nb, nvh, lw)
    hm_spec = pl.BlockSpec((nvh, lw, LANES), lambda b, j: (0, b * nq + j, 0))
    if hp == 1:
        out_spec = pl.BlockSpec((tq, ATTN_WIDTH), lambda b, j: (b * nq + j, 0))
        out_shape = jax.ShapeDtypeStruct((N, ATTN_WIDTH), bf16)
    else:
        out_spec = pl.BlockSpec((1, nvh * HEAD_DIM, lw), lambda b, j: (b * nq + j, 0, 0))
        out_shape = jax.ShapeDtypeStruct((nb, nvh * HEAD_DIM, lw), bf16)
    out = pl.pallas_call(
        functools.partial(_attn_kernel, tq=tq, hp=hp, ts=ts, t1=128, ta=ta, past=past, topk=topk),
        grid=(B, nq),
        in_specs=[
            hm_spec,
            pl.BlockSpec((1, nvh, lw), lambda b, j: (b * nq + j, 0, 0)),
            hm_spec,
            pl.BlockSpec((1, kwt_all.shape[1], s_pad), lambda b, j: (b, 0, 0)),
        ] + [pl.BlockSpec((1, LANES, s_pad), lambda b, j: (b, 0, 0))] * 2,
        out_specs=out_spec,
        out_shape=out_shape,
        scratch_shapes=[pltpu.VMEM((s_pad, lw2), f32), pltpu.VMEM((nvh, 1, lw), f32),
                        pltpu.VMEM((nvh, V_ROWS, lw), f32), pltpu.VMEM((nvh * HEAD_DIM, lw), f32),
                        pltpu.VMEM((2, nvh, ta, lw), f32), pltpu.VMEM((2, nvh, 1, lw), f32),
                        pltpu.VMEM((SUBLANES, lw2), f32),
                        pltpu.VMEM((32, s_pad // ts, SUBLANES, lw2), jnp.int32),
                        pltpu.VMEM((s_pad // ts, SUBLANES, lw2), jnp.int32)],
        compiler_params=pltpu.CompilerParams(dimension_semantics=("arbitrary", "arbitrary"),
                                             vmem_limit_bytes=VMEM_LIMIT),
        name="dsa_attention",
    )(pack(qi_hm), wi3, pack(q_hm), kwt_all, kt_all, vt_all)
    if hp > 1:
        out = out.reshape(nb, nvh, HEAD_DIM, hp, tq)
        out = jnp.transpose(out, (0, 4, 1, 3, 2)).reshape(N, ATTN_WIDTH)
    return out


FF_CHUNK = 256


def _post_kernel(x_ref, c_ref, a_ref, wo_ref, g1_ref, b1_ref, wgu_ref, wd_ref, g2_ref, b2_ref, o_ref, acc_ref):
    w16 = lambda r, rows, cols: r[0, rows, cols].astype(bf16)
    full = slice(None)
    mix = (jnp.dot(c_ref[...], w16(wo_ref, slice(0, C_CONV), full), preferred_element_type=f32)
           + jnp.dot(a_ref[...], w16(wo_ref, slice(C_CONV, D_MODEL), full), preferred_element_type=f32))
    x1 = _layer_norm(ALPHA * x_ref[...] + mix, g1_ref[...], b1_ref[...])
    x1b = x1.astype(bf16)
    for c in range(D_FF // FF_CHUNK):
        lo = c * FF_CHUNK
        gate = jnp.dot(x1b, w16(wgu_ref, full, slice(lo, lo + FF_CHUNK)), preferred_element_type=f32)
        up = jnp.dot(x1b, w16(wgu_ref, full, slice(D_FF + lo, D_FF + lo + FF_CHUNK)), preferred_element_type=f32)
        act = (gate * jax.nn.sigmoid(gate) * up).astype(bf16)
        part = jnp.dot(act, w16(wd_ref, slice(lo, lo + FF_CHUNK), full), preferred_element_type=f32)
        if c == 0:
            acc_ref[...] = part
        else:
            acc_ref[...] += part
    o_ref[...] = _layer_norm(ALPHA * x1 + acc_ref[...], g2_ref[...], b2_ref[...])


def _post(x2d, conv2d, attn2d, layer, w_o, g1, b1, w_gate_up, w_down, g2, b2):
    N = x2d.shape[0]
    tm = min(512, N)
    row = lambda w: pl.BlockSpec((tm, w), lambda i: (i, 0))
    vec_spec = pl.BlockSpec((1, D_MODEL), lambda i: (0, 0))
    weight = lambda a: pl.BlockSpec((1,) + a.shape[1:], lambda i: (layer, 0, 0), pipeline_mode=pl.Buffered(1))
    vec = lambda a: a.reshape(1, D_MODEL)
    return pl.pallas_call(
        _post_kernel,
        grid=(N // tm,),
        in_specs=[row(D_MODEL), row(C_CONV), row(ATTN_WIDTH), weight(w_o), vec_spec, vec_spec,
                  weight(w_gate_up), weight(w_down), vec_spec, vec_spec],
        out_specs=row(D_MODEL),
        out_shape=jax.ShapeDtypeStruct((N, D_MODEL), f32),
        scratch_shapes=[pltpu.VMEM((tm, D_MODEL), f32)],
        compiler_params=pltpu.CompilerParams(dimension_semantics=("arbitrary",), vmem_limit_bytes=VMEM_LIMIT),
        name="post",
    )(x2d, conv2d, attn2d, w_o, vec(g1), vec(b1), w_gate_up, w_down, vec(g2), vec(b2))


def _layer(x, conv_state, k_past, v_past, ki_past, wts):
    (layer, w_in, w_tail, conv_w, conv_b, cln_g, cln_b, w_o, ln1_g, ln1_b, w_gate_up, w_down, ln2_g, ln2_b) = wts
    B, T, _ = x.shape
    past = k_past.shape[1]
    N = B * T
    x2d = x.reshape(N, D_MODEL)
    u, q_hm, qi_hm, kt, vt, kwt = _inproj(x2d, layer, w_in, w_tail, T, past)

    u3 = u.reshape(B, T, C_CONV)
    conv_out = _conv(u3, conv_state, conv_w, conv_b, cln_g, cln_b)
    new_conv = jnp.concatenate([conv_state.astype(f32), u3], axis=1)[:, -(CONV_WIDTH - 1):]

    per_head = lambda a: jnp.transpose(a.reshape(B, N_KV_HEADS, HEAD_DIM, T), (0, 3, 1, 2))
    k, v = per_head(kt), per_head(vt)
    ki = jnp.transpose(kwt[:, :IDX_DIM, :], (0, 2, 1))
    wi2d = jnp.transpose(kwt[:, IDX_DIM:IDX_DIM + N_IDX_HEADS, :], (0, 2, 1)).reshape(N, N_IDX_HEADS)
    S = past + T
    s_pad = -(-S // 256) * 256
    if past == 0 and s_pad == T:
        kwt_all, kt_all, vt_all = kwt, kt, vt
    else:
        def with_cache(old, new):
            old = jnp.moveaxis(old.reshape(B, past, -1), 1, 2)
            rows = old.shape[1]
            return jnp.concatenate([old, new[:, :rows, :], jnp.zeros((B, rows, s_pad - S), f32)], axis=2)

        kwt_all, kt_all, vt_all = with_cache(ki_past, kwt), with_cache(k_past, kt), with_cache(v_past, vt)
    attn = _attention(qi_hm, wi2d, q_hm, kwt_all, kt_all, vt_all, B=B, T=T, past=past, topk=min(TOPK_MAX, S // 4))

    y = _post(x2d, conv_out.reshape(N, C_CONV), attn, layer, w_o, ln1_g, ln1_b, w_gate_up, w_down, ln2_g, ln2_b)
    return y.reshape(B, T, D_MODEL), k, v, ki, new_conv


def kernel(x_prompt, x_sample, cache_k, cache_v, cache_k_idx, state_conv, w_in, conv_w, conv_b, conv_ln_g,
           conv_ln_b, w_o, ln1_g, ln1_b, w_gate_up, w_down, ln2_g, ln2_b):
    Bp = x_prompt.shape[0]
    dt = x_prompt.dtype
    empty_kv = jnp.zeros((Bp, 0, N_KV_HEADS, HEAD_DIM), dt)
    empty_ki = jnp.zeros((Bp, 0, IDX_DIM), dt)
    zero_conv = jnp.zeros((Bp, CONV_WIDTH - 1, C_CONV), dt)
    hp, hs = x_prompt, x_sample
    outs_p, outs_s = [], []
    for l in range(DEPTH):
        wts = (
            l, w_in, jnp.pad(w_in[l, :, _O_KW:].astype(bf16), ((0, 0), (0, _O_KW + LANES - IN_DIM))),
            conv_w[l], conv_b[l], conv_ln_g[l], conv_ln_b[l],
            w_o, ln1_g[l], ln1_b[l], w_gate_up, w_down, ln2_g[l], ln2_b[l],
        )
        hp, *op = _layer(hp, zero_conv, empty_kv, empty_kv, empty_ki, wts)
        hs, *os_ = _layer(hs, state_conv[l], cache_k[l], cache_v[l], cache_k_idx[l], wts)
        outs_p.append(op)
        outs_s.append(os_)
    stack = lambda outs, i: jnp.stack([o[i] for o in outs])
    return (hp, hs,
            stack(outs_p, 0), stack(outs_p, 1), stack(outs_p, 2), stack(outs_p, 3),
            stack(outs_s, 0), stack(outs_s, 1), stack(outs_s, 2), stack(outs_s, 3))
```

```python
import functools
import math

import jax
import jax.numpy as jnp
from jax import lax
from jax.experimental import pallas as pl
from jax.experimental.pallas import tpu as pltpu

D_MODEL = 1024
CHUNK = 64
C_CONV = D_MODEL // 2
CONV_WIDTH = 31
HEAD_DIM = 64
N_HEADS = (D_MODEL // 2) // HEAD_DIM
N_KV_HEADS = 2
GROUP = N_HEADS // N_KV_HEADS
ATTN_WIDTH = N_HEADS * HEAD_DIM
N_IDX_HEADS = 8
IDX_DIM = 64
TOPK_MAX = 256
ROPE_THETA = 10000.0
D_FF = -(-8 * D_MODEL // (3 * 256)) * 256
IN_DIM = 2 * C_CONV + (N_HEADS + 2 * N_KV_HEADS) * HEAD_DIM + N_IDX_HEADS * IDX_DIM + IDX_DIM + N_IDX_HEADS
DEPTH = 2
ALPHA = (2 * DEPTH) ** 0.25
ATTN_SCALE = HEAD_DIM ** -0.5
IDX_SCALE = IDX_DIM ** -0.5
IDX_HEAD_SCALE = N_IDX_HEADS ** -0.5
LN_EPS = 1e-5
Q_SCALE = ATTN_SCALE * math.log2(math.e)

LANES = 128
SUBLANES = 8
VMEM_LIMIT = 56 * 1024 * 1024
IN_PAD = -(-IN_DIM // LANES) * LANES
HALO = 32
NEG_BIG = -1e30
INT_MIN = -(2 ** 31)
F32_LOWEST = float(jnp.finfo(jnp.float32).min)
NEG_INF_WORD = 0x007FFFFF
V_ROWS = HEAD_DIM + 16

_O_A = 0
_O_G = _O_A + C_CONV
_O_Q = _O_G + C_CONV
_O_K = _O_Q + ATTN_WIDTH
_O_V = _O_K + N_KV_HEADS * HEAD_DIM
_O_QI = _O_V + N_KV_HEADS * HEAD_DIM
_O_KW = _O_QI + N_IDX_HEADS * IDX_DIM

_NT = (((1,), (1,)), ((), ()))

f32 = jnp.float32
bf16 = jnp.bfloat16


def _layer_norm(y, g, b):
    mu = jnp.mean(y, axis=-1, keepdims=True)
    d = y - mu
    var = jnp.mean(d * d, axis=-1, keepdims=True)
    return d * lax.rsqrt(var + LN_EPS) * g + b


def _rope128(x, cos, sa, sb):
    return x * cos + pltpu.roll(x, 96, 1) * sa + pltpu.roll(x, 32, 1) * sb


def _inproj_kernel(x_ref, w_ref, wt_ref, cos_ref, sa_ref, sb_ref, cosk_ref, sak_ref, sbk_ref,
                   u_ref, q_ref, qi_ref, kt_ref, vt_ref, kwt_ref, *, seq):
    xb = x_ref[...].astype(bf16)
    tm = xb.shape[0]

    def put_t(ref, a):
        at = a.T
        for s in range(ref.shape[0]):
            ref[s] = at[:, s * seq:(s + 1) * seq] if tm > seq else at

    def mm(lo, width):
        return jnp.dot(xb, w_ref[0, :, lo:lo + width].astype(bf16), preferred_element_type=f32)

    u_ref[...] = mm(_O_A, C_CONV) * jax.nn.sigmoid(mm(_O_G, C_CONV))

    cos, sa, sb = cos_ref[...], sa_ref[...], sb_ref[...]
    low = lax.broadcasted_iota(jnp.int32, (tm, LANES), 1) < HEAD_DIM

    def place(r, src, dst):
        if src != dst:
            r = pltpu.roll(r, HEAD_DIM, 1)
        return jnp.where(low if dst == 0 else jnp.logical_not(low), r, 0.0).astype(bf16)

    q = mm(_O_Q, ATTN_WIDTH)
    qi = mm(_O_QI, N_IDX_HEADS * IDX_DIM)
    for c in range(ATTN_WIDTH // LANES):
        r = _rope128(q[:, c * LANES:(c + 1) * LANES], cos, sa, sb) * Q_SCALE
        for half in range(2):
            h = 2 * c + half
            q_ref[h] = place(r, half, h // GROUP)
        r = _rope128(qi[:, c * LANES:(c + 1) * LANES], cos, sa, sb)
        for half in range(2):
            qi_ref[2 * c + half] = place(r, half, 0)
    put_t(kt_ref, _rope128(mm(_O_K, N_KV_HEADS * HEAD_DIM), cos, sa, sb))
    put_t(vt_ref, mm(_O_V, N_KV_HEADS * HEAD_DIM))
    kw = jnp.dot(xb, wt_ref[...], preferred_element_type=f32)
    put_t(kwt_ref, _rope128(kw, cosk_ref[...], sak_ref[...], sbk_ref[...]))


def _rope_tables(T, past, rows):
    half = HEAD_DIM // 2
    inv = ROPE_THETA ** (-jnp.arange(half, dtype=f32) / half)
    pos = (jnp.arange(T, dtype=jnp.int32) + past).astype(f32)
    ang = pos[:, None] * inv[None, :]
    cos, sin = jnp.cos(ang), jnp.sin(ang)
    zero = jnp.zeros_like(sin)
    cos64 = jnp.concatenate([cos, cos], axis=1)
    sa64 = jnp.concatenate([-sin, zero], axis=1)
    sb64 = jnp.concatenate([zero, sin], axis=1)
    two = lambda a: jnp.concatenate([a, a], axis=1)
    wscale = jnp.full((T, N_IDX_HEADS), IDX_HEAD_SCALE * IDX_SCALE, f32)
    ones = jnp.ones((T, LANES - IDX_DIM - N_IDX_HEADS), f32)
    cosk = jnp.concatenate([cos64, wscale, ones], axis=1)
    zpad = jnp.zeros((T, LANES - IDX_DIM), f32)
    tabs = (two(cos64), two(sa64), two(sb64), cosk,
            jnp.concatenate([sa64, zpad], axis=1), jnp.concatenate([sb64, zpad], axis=1))
    reps = rows // T
    return tuple(jnp.tile(t, (reps, 1)) for t in tabs)


def _inproj(x2d, layer, w_in, w_tail, T, past):
    N = x2d.shape[0]
    tm = min(512, N)
    rows = max(T, tm)
    tabs = _rope_tables(T, past, rows)
    nt = rows // tm
    row_spec = lambda w: pl.BlockSpec((tm, w), lambda i: (i, 0))
    tab_spec = pl.BlockSpec((tm, LANES), lambda i: (i % nt, 0))
    hm_spec = pl.BlockSpec((N_HEADS, tm, LANES), lambda i: (0, i, 0))
    if tm <= T:
        per = T // tm
        t_spec = pl.BlockSpec((1, LANES, tm), lambda i: (i // per, 0, i % per))
    else:
        t_spec = pl.BlockSpec((tm // T, LANES, T), lambda i: (i, 0, 0))
    t_shape = jax.ShapeDtypeStruct((N // T, LANES, T), f32)
    return pl.pallas_call(
        functools.partial(_inproj_kernel, seq=T),
        grid=(N // tm,),
        in_specs=[row_spec(D_MODEL),
                  pl.BlockSpec((1, D_MODEL, IN_DIM), lambda i: (layer, 0, 0), pipeline_mode=pl.Buffered(1)),
                  pl.BlockSpec((D_MODEL, LANES), lambda i: (0, 0))] + [tab_spec] * 6,
        out_specs=[row_spec(C_CONV), hm_spec, hm_spec, t_spec, t_spec, t_spec],
        out_shape=[
            jax.ShapeDtypeStruct((N, C_CONV), f32),
            jax.ShapeDtypeStruct((N_HEADS, N, LANES), bf16),
            jax.ShapeDtypeStruct((N_IDX_HEADS, N, LANES), bf16),
            t_shape, t_shape, t_shape,
        ],
        compiler_params=pltpu.CompilerParams(dimension_semantics=("arbitrary",), vmem_limit_bytes=VMEM_LIMIT),
        name="inproj",
    )(x2d, w_in, w_tail, *tabs)


def _conv_kernel(u_ref, halo_ref, st_ref, w_ref, b_ref, g_ref, bt_ref, o_ref, f_ref, sh_ref, *, tc, rb):
    i = pl.program_id(1)
    f_ref[0:HALO, :] = jnp.where(i == 0, st_ref[0], halo_ref[0])
    f_ref[HALO:HALO + tc, :] = u_ref[0]
    span = tc + HALO - SUBLANES
    for c in range(1, SUBLANES):
        sh_ref[c, 0:span, :] = f_ref[c:c + span, :]
    lead = HALO - (CONV_WIDTH - 1)
    for r in range(tc // rb):
        acc = jnp.zeros((rb, C_CONV), f32)
        for j in range(CONV_WIDTH):
            c = (lead + j) % SUBLANES
            lo = r * rb + lead + j - c
            rows = f_ref[lo:lo + rb, :] if c == 0 else sh_ref[c, lo:lo + rb, :]
            acc = acc + rows * jnp.tile(w_ref[j], (rb // SUBLANES, 1))
        y = _layer_norm(acc + b_ref[...], g_ref[...], bt_ref[...])
        o_ref[0, r * rb:(r + 1) * rb, :] = (y * jax.nn.sigmoid(y)).astype(o_ref.dtype)


def _conv(u3, state, conv_w, conv_b, ln_g, ln_b):
    B, T, _ = u3.shape
    tc = min(256, T)
    st = jnp.pad(state.astype(f32), ((0, 0), (HALO - (CONV_WIDTH - 1), 0), (0, 0)))
    wp = jnp.broadcast_to(conv_w[:, None, :], (CONV_WIDTH, SUBLANES, C_CONV))
    vec = lambda a: a.reshape(1, C_CONV)
    hb = tc // HALO
    return pl.pallas_call(
        functools.partial(_conv_kernel, tc=tc, rb=32),
        grid=(B, T // tc),
        in_specs=[
            pl.BlockSpec((1, tc, C_CONV), lambda b, i: (b, i, 0)),
            pl.BlockSpec((1, HALO, C_CONV), lambda b, i: (b, jnp.maximum(i * hb - 1, 0), 0)),
            pl.BlockSpec((1, HALO, C_CONV), lambda b, i: (b, 0, 0)),
            pl.BlockSpec((CONV_WIDTH, SUBLANES, C_CONV), lambda b, i: (0, 0, 0)),
        ] + [pl.BlockSpec((1, C_CONV), lambda b, i: (0, 0))] * 3,
        out_specs=pl.BlockSpec((1, tc, C_CONV), lambda b, i: (b, i, 0)),
        out_shape=jax.ShapeDtypeStruct((B, T, C_CONV), bf16),
        scratch_shapes=[pltpu.VMEM((HALO + tc, C_CONV), f32),
                        pltpu.VMEM((SUBLANES, HALO + tc - SUBLANES, C_CONV), f32)],
        compiler_params=pltpu.CompilerParams(dimension_semantics=("arbitrary", "arbitrary"),
                                             vmem_limit_bytes=VMEM_LIMIT),
        name="conv",
    )(u3, u3, st, wp, vec(conv_b), vec(ln_g), vec(ln_b))


def _key_to_f32(key):
    bits = key ^ ((key >> 31) & jnp.int32(0x7FFFFFFF))
    return pltpu.bitcast(bits, f32)


def _attn_kernel(qi_ref, wi_ref, q_ref, kwt_ref, kt_ref, vt_ref, o_ref, sc_ref, m_ref, acc_ref, ot_ref, s_ref,
                 mx_ref, st_ref, planes_ref, cand_ref,
                 *, tq, hp, ts, t1, ta, past, topk):
    nvh = N_HEADS // hp
    lw = hp * tq
    lw2 = sc_ref.shape[1]
    j = pl.program_id(1)
    n_full = (past + j * tq) // ts
    n_chunks = n_full + 1
    kf = float(topk)

    def key_rows(ref, row0, n):
        x = ref[0, :, pl.ds(row0, n)].astype(f32)
        if x.shape[0] < LANES:
            x = jnp.concatenate([x, jnp.zeros((LANES - x.shape[0], n), f32)], axis=0)
        return x.T.astype(bf16)

    ones_rows = jnp.where(lax.broadcasted_iota(jnp.int32, (V_ROWS - HEAD_DIM, ta), 0) == 0, 1.0, 0.0).astype(bf16)

    q_chunk = (lax.broadcasted_iota(jnp.int32, (t1, lw2), 1) % tq) // CHUNK

    nc_max = cand_ref.shape[0]
    int_min = jnp.int32(INT_MIN)
    cand_ref[...] = jnp.zeros(cand_ref.shape, jnp.int32)

    def write_planes(c):
        row0 = pl.multiple_of(c * ts, ts)
        for lt in range(lw2 // LANES):
            lanes = slice(lt * LANES, (lt + 1) * LANES)
            bits = pltpu.bitcast(sc_ref[pl.ds(row0, ts), lanes], jnp.int32)
            u = bits ^ ((bits >> 31) | int_min)
            w = [u[i * SUBLANES:(i + 1) * SUBLANES, :] for i in range(32)]
            j, m = 16, 0x0000FFFF
            while j:
                k = 0
                while k < 32:
                    t = (w[k] ^ (w[k + j] >> j)) & jnp.int32(m)
                    w[k] = w[k] ^ t
                    w[k + j] = w[k + j] ^ (t << j)
                    k = (k + j + 1) & ~j
                j >>= 1
                m = m ^ (m << j)
            gone = jnp.full((SUBLANES, LANES), -1, jnp.int32)
            for b in range(32):
                plane = w[31 - b]
                planes_ref[b, c, :, lanes] = plane
                gone = gone & (plane if (NEG_INF_WORD >> b) & 1 else ~plane)
            cand_ref[c, :, lanes] = ~gone

    def score_chunk(c, masked):
        for r in range(ts // t1):
            row0 = pl.multiple_of(c * ts + r * t1, t1)
            kic = key_rows(kwt_ref, row0, t1)
            acc = jnp.zeros((t1, lw), f32)
            for hv in range(nvh):
                d = lax.dot_general(kic, qi_ref[hv], _NT, preferred_element_type=f32)
                acc = acc + jnp.maximum(d, 0.0) * wi_ref[0, hv:hv + 1, :]
            width = lw
            while width > tq:
                width //= 2
                acc = acc + pltpu.roll(acc, width, 1)
            acc = acc[:, :lw2]
            if masked:
                k_chunk = (lax.broadcasted_iota(jnp.int32, (t1, lw2), 0) + r * t1) // CHUNK
                acc = jnp.where(k_chunk <= q_chunk, acc, -jnp.inf)
            sc_ref[pl.ds(row0, t1), :] = acc
        write_planes(c)

    def pair_body(i, carry):
        score_chunk(2 * i, False)
        score_chunk(2 * i + 1, False)
        return carry

    lax.fori_loop(0, n_full // 2, pair_body, 0)

    @pl.when(n_full % 2 == 1)
    def _():
        score_chunk(n_full - 1, False)

    score_chunk(n_full, True)

    def count(preds):
        def body(c, accs):
            row0 = pl.multiple_of(c * ts, ts)
            blk = sc_ref[pl.ds(row0, ts), :]
            out = []
            for pred, acc in zip(preds, accs):
                m = jnp.where(pred(blk, row0), 1.0, 0.0)
                out.append(acc + m.reshape(ts // SUBLANES, SUBLANES, lw2).sum(axis=0))
            return tuple(out)
        init = tuple(jnp.zeros((SUBLANES, lw2), f32) for _ in preds)
        accs = lax.fori_loop(0, n_chunks, body, init)
        return [a.sum(axis=0, keepdims=True) for a in accs]

    def cut_counts(thr):
        c_ge, c_gt = count([lambda blk, row0: blk >= thr, lambda blk, row0: blk > thr])
        st_ref[0:1, :] = thr
        st_ref[1:2, :] = c_ge
        st_ref[2:3, :] = c_gt

    def idle_body(c, carry):
        planes_ref[:, c] = jnp.zeros((32, SUBLANES, lw2), jnp.int32)
        return carry

    lax.fori_loop(n_chunks, nc_max, idle_body, 0)

    def bit_count(b, flip):
        cnt = jnp.zeros((SUBLANES, lw2), jnp.int32)
        tot = jnp.zeros((SUBLANES, lw2), jnp.int32)
        for c in range(nc_max):
            cand = cand_ref[c]
            if flip is None:
                tot = tot + lax.population_count(cand)
            else:
                cand = cand & (planes_ref[b + 1, c] ^ flip)
                cand_ref[c] = cand
            cnt = cnt + lax.population_count(cand & planes_ref[b, c])
        red = lambda a: a.astype(f32).sum(axis=0, keepdims=True)
        return red(cnt), red(tot)

    def decide(c1, left, word, b):
        take = c1 >= left
        return (jnp.where(take, left, left - c1), word | jnp.where(take, lax.shift_left(jnp.int32(1), b), 0),
                jnp.where(take, 0, -1))

    c1, n_adm = bit_count(31, None)
    state = decide(c1, jnp.full((1, lw2), kf, f32), jnp.zeros((1, lw2), jnp.int32), 31)

    def radix_body(it, state):
        left, word, flip = state
        b = 30 - it
        c1, _ = bit_count(b, flip)
        return decide(c1, left, word, b)

    _, word, _ = lax.fori_loop(0, 31, radix_body, state)
    few = n_adm < kf
    cut_counts(jnp.where(few, F32_LOWEST, _key_to_f32(word ^ int_min)))

    good = jnp.logical_or(few, jnp.logical_and(st_ref[2:3, :] < kf, st_ref[1:2, :] >= kf))

    @pl.when(jnp.max(jnp.where(good, 0.0, 1.0)) > 0.0)
    def _():
        def bisect(it, key):
            cand = key + lax.shift_left(jnp.int32(1), 31 - it)
            thr = _key_to_f32(cand)
            cnt, = count([lambda blk, row0: blk >= thr])
            return jnp.where(cnt >= kf, cand, key)

        key = lax.fori_loop(0, 32, bisect, jnp.full((1, lw2), INT_MIN, jnp.int32))
        cut_counts(jnp.where(key == INT_MIN, F32_LOWEST, _key_to_f32(key)))

    thr, c_ge, c_gt = st_ref[0:1, :], st_ref[1:2, :], st_ref[2:3, :]
    ties = c_ge > kf
    need = kf - c_gt
    any_ties = jnp.max(jnp.where(ties, 1.0, 0.0))

    def write_bias(select):
        def body(c, carry):
            row0 = pl.multiple_of(c * ts, ts)
            blk = sc_ref[pl.ds(row0, ts), :]
            sc_ref[pl.ds(row0, ts), :] = jnp.where(select(blk, row0), 0.0, NEG_BIG)
            return carry
        lax.fori_loop(0, n_chunks, body, 0)

    @pl.when(any_ties == 0.0)
    def _():
        write_bias(lambda blk, row0: blk >= thr)

    @pl.when(any_ties > 0.0)
    def _():
        def rows(row0):
            return lax.broadcasted_iota(jnp.int32, (ts, lw2), 0) + row0

        tb = pltpu.bitcast(thr, jnp.int32)
        tw = tb ^ ((tb >> 31) | int_min)
        cand_ref[...] = jnp.full(cand_ref.shape, -1, jnp.int32)

        def tied_body(b, carry):
            flip = jnp.where(((tw >> b) & 1) == 1, 0, -1)
            for c in range(nc_max):
                cand_ref[c] = cand_ref[c] & (planes_ref[b, c] ^ flip)
            return carry

        lax.fori_loop(0, 32, tied_body, 0)

        sub = lax.broadcasted_iota(jnp.int32, (SUBLANES, lw2), 0)
        passes = [("chunk", k) for k in reversed(range(max(1, (nc_max - 1).bit_length())))]
        passes += [("group", k) for k in reversed(range(5))] + [("sublane", k) for k in reversed(range(3))]
        left, cut = need, jnp.zeros((1, lw2), jnp.int32)
        for kind, k in passes:
            def high(c, kind=kind, k=k):
                if kind == "chunk":
                    return jnp.int32(-1 if (c >> k) & 1 else 0)
                if kind == "group":
                    word = sum(1 << i for i in range(32) if ((31 - i) >> k) & 1)
                    return jnp.int32(word - (1 << 32) if word >= (1 << 31) else word)
                return jnp.where(((sub >> k) & 1) == 1, -1, 0)
            cnt = jnp.zeros((SUBLANES, lw2), jnp.int32)
            for c in range(nc_max):
                cnt = cnt + lax.population_count(cand_ref[c] & ~high(c))
            c0 = cnt.astype(f32).sum(axis=0, keepdims=True)
            low = c0 >= left
            left = jnp.where(low, left, left - c0)
            weight = {"chunk": 8, "group": 3, "sublane": 0}[kind] + k
            cut = cut | jnp.where(low, 0, 1 << weight)
            flip = jnp.where(low, -1, 0)
            for c in range(nc_max):
                cand_ref[c] = cand_ref[c] & (high(c) ^ flip)
        cut = jnp.where(ties, cut, jnp.int32(2 ** 30))
        write_bias(lambda blk, row0: jnp.logical_or(
            blk > thr, jnp.logical_and(blk == thr, rows(row0) <= cut)))

    m_ref[...] = jnp.full(m_ref.shape, NEG_BIG, f32)
    acc_ref[...] = jnp.zeros(acc_ref.shape, f32)

    n_steps = n_chunks * (ts // ta)
    last = n_steps - 1

    def logits(step, buf):
        row0 = pl.multiple_of(step * ta, ta)
        kc = key_rows(kt_ref, row0, ta)
        for hv in range(nvh):
            bias = sc_ref[pl.ds(row0, ta), :]
            if lw2 < lw:
                bias = jnp.concatenate([bias] * (lw // lw2), axis=1)
            s = lax.dot_general(kc, q_ref[hv], _NT, preferred_element_type=f32) + bias
            s_ref[buf, hv] = s
            mx_ref[buf, hv] = s.reshape(ta // SUBLANES, SUBLANES, lw).max(axis=0).max(axis=0, keepdims=True)

    def absorb(step, buf):
        row0 = pl.multiple_of(step * ta, ta)
        for hv in range(nvh):
            g = (hv * hp) // GROUP
            m_old = m_ref[hv]
            m_new = jnp.maximum(m_old, mx_ref[buf, hv])
            alpha = jnp.exp2(m_old - m_new)
            p = jnp.exp2(s_ref[buf, hv] - m_new).astype(bf16)
            vt = jnp.concatenate(
                [vt_ref[0, g * HEAD_DIM:(g + 1) * HEAD_DIM, pl.ds(row0, ta)].astype(bf16), ones_rows], axis=0)
            acc_ref[hv] = alpha * acc_ref[hv] + jnp.dot(vt, p, preferred_element_type=f32)
            m_ref[hv] = m_new

    logits(0, 0)

    def att_body(i, carry):
        logits(jnp.minimum(2 * i + 1, last), 1)
        absorb(2 * i, 0)
        logits(jnp.minimum(2 * i + 2, last), 0)
        absorb(2 * i + 1, 1)
        return carry

    lax.fori_loop(0, n_steps // 2, att_body, 0)

    @pl.when(n_steps % 2 == 1)
    def _():
        absorb(last, 0)

    for hv in range(nvh):
        o = acc_ref[hv]
        ot_ref[hv * HEAD_DIM:(hv + 1) * HEAD_DIM, :] = o[:HEAD_DIM] / o[HEAD_DIM:HEAD_DIM + 1]
    if hp == 1:
        o_ref[...] = ot_ref[...].T.astype(o_ref.dtype)
    else:
        o_ref[0] = ot_ref[...].astype(o_ref.dtype)


def _attention(qi_hm, wi2d, q_hm, kwt_all, kt_all, vt_all, *, B, T, past, topk):
    tq = min(256, T)
    ts = 256
    ta = 256
    hp = ts // tq
    assert past % ts == 0 and (tq == ts or tq == T) and tq % CHUNK == 0 and hp in (1, 2, 4)
    nvh = N_HEADS // hp
    lw = hp * tq
    lw2 = max(tq, LANES)
    nq = T // tq
    nb = B * nq
    s_pad = (past + T + ts - 1) // ts * ts
    assert kt_all.shape == vt_all.shape == (B, LANES, s_pad)
    assert kwt_all.shape in ((B, LANES, s_pad), (B, IDX_DIM, s_pad))
    N = B * T

    def pack(a):
        if hp == 1:
            return a
        a = a.reshape(nvh, hp, nb, tq, LANES)
        return jnp.transpose(a, (0, 2, 1, 3, 4)).reshape(nvh, nb * lw, LANES)

    wi3 = jnp.transpose(wi2d.reshape(nb, tq, nvh, hp), (0, 2, 3, 1)).reshape(nb, nvh, lw)
    hm_spec = pl.BlockSpec((nvh, lw, LANES), lambda b, j: (0, b * nq + j, 0))
    if hp == 1:
        out_spec = pl.BlockSpec((tq, ATTN_WIDTH), lambda b, j: (b * nq + j, 0))
        out_shape = jax.ShapeDtypeStruct((N, ATTN_WIDTH), bf16)
    else:
        out_spec = pl.BlockSpec((1, nvh * HEAD_DIM, lw), lambda b, j: (b * nq + j, 0, 0))
        out_shape = jax.ShapeDtypeStruct((nb, nvh * HEAD_DIM, lw), bf16)
    out = pl.pallas_call(
        functools.partial(_attn_kernel, tq=tq, hp=hp, ts=ts, t1=128, ta=ta, past=past, topk=topk),
        grid=(B, nq),
        in_specs=[
            hm_spec,
            pl.BlockSpec((1, nvh, lw), lambda b, j: (b * nq + j, 0, 0)),
            hm_spec,
            pl.BlockSpec((1, kwt_all.shape[1], s_pad), lambda b, j: (b, 0, 0)),
        ] + [pl.BlockSpec((1, LANES, s_pad), lambda b, j: (b, 0, 0))] * 2,
        out_specs=out_spec,
        out_shape=out_shape,
        scratch_shapes=[pltpu.VMEM((s_pad, lw2), f32), pltpu.VMEM((nvh, 1, lw), f32),
                        pltpu.VMEM((nvh, V_ROWS, lw), f32), pltpu.VMEM((nvh * HEAD_DIM, lw), f32),
                        pltpu.VMEM((2, nvh, ta, lw), f32), pltpu.VMEM((2, nvh, 1, lw), f32),
                        pltpu.VMEM((SUBLANES, lw2), f32),
                        pltpu.VMEM((32, s_pad // ts, SUBLANES, lw2), jnp.int32),
                        pltpu.VMEM((s_pad // ts, SUBLANES, lw2), jnp.int32)],
        compiler_params=pltpu.CompilerParams(dimension_semantics=("arbitrary", "arbitrary"),
                                             vmem_limit_bytes=VMEM_LIMIT),
        name="dsa_attention",
    )(pack(qi_hm), wi3, pack(q_hm), kwt_all, kt_all, vt_all)
    if hp > 1:
        out = out.reshape(nb, nvh, HEAD_DIM, hp, tq)
        out = jnp.transpose(out, (0, 4, 1, 3, 2)).reshape(N, ATTN_WIDTH)
    return out


FF_CHUNK = 256


def _post_kernel(x_ref, c_ref, a_ref, wo_ref, g1_ref, b1_ref, wgu_ref, wd_ref, g2_ref, b2_ref, o_ref, acc_ref):
    w16 = lambda r, rows, cols: r[0, rows, cols].astype(bf16)
    full = slice(None)
    mix = (jnp.dot(c_ref[...], w16(wo_ref, slice(0, C_CONV), full), preferred_element_type=f32)
           + jnp.dot(a_ref[...], w16(wo_ref, slice(C_CONV, D_MODEL), full), preferred_element_type=f32))
    x1 = _layer_norm(ALPHA * x_ref[...] + mix, g1_ref[...], b1_ref[...])
    x1b = x1.astype(bf16)
    for c in range(D_FF // FF_CHUNK):
        lo = c * FF_CHUNK
        gate = jnp.dot(x1b, w16(wgu_ref, full, slice(lo, lo + FF_CHUNK)), preferred_element_type=f32)
        up = jnp.dot(x1b, w16(wgu_ref, full, slice(D_FF + lo, D_FF + lo + FF_CHUNK)), preferred_element_type=f32)
        act = (gate * jax.nn.sigmoid(gate) * up).astype(bf16)
        part = jnp.dot(act, w16(wd_ref, slice(lo, lo + FF_CHUNK), full), preferred_element_type=f32)
        if c == 0:
            acc_ref[...] = part
        else:
            acc_ref[...] += part
    o_ref[...] = _layer_norm(ALPHA * x1 + acc_ref[...], g2_ref[...], b2_ref[...])


def _post(x2d, conv2d, attn2d, layer, w_o, g1, b1, w_gate_up, w_down, g2, b2):
    N = x2d.shape[0]
    tm = min(512, N)
    row = lambda w: pl.BlockSpec((tm, w), lambda i: (i, 0))
    vec_spec = pl.BlockSpec((1, D_MODEL), lambda i: (0, 0))
    weight = lambda a: pl.BlockSpec((1,) + a.shape[1:], lambda i: (layer, 0, 0), pipeline_mode=pl.Buffered(1))
    vec = lambda a: a.reshape(1, D_MODEL)
    return pl.pallas_call(
        _post_kernel,
        grid=(N // tm,),
        in_specs=[row(D_MODEL), row(C_CONV), row(ATTN_WIDTH), weight(w_o), vec_spec, vec_spec,
                  weight(w_gate_up), weight(w_down), vec_spec, vec_spec],
        out_specs=row(D_MODEL),
        out_shape=jax.ShapeDtypeStruct((N, D_MODEL), f32),
        scratch_shapes=[pltpu.VMEM((tm, D_MODEL), f32)],
        compiler_params=pltpu.CompilerParams(dimension_semantics=("arbitrary",), vmem_limit_bytes=VMEM_LIMIT),
        name="post",
    )(x2d, conv2d, attn2d, w_o, vec(g1), vec(b1), w_gate_up, w_down, vec(g2), vec(b2))


def _join_kernel(*refs, past, T):
    n = len(refs) // 3
    for c_ref, n_ref, o_ref in zip(refs[:n], refs[n:2 * n], refs[2 * n:]):
        rows, width = o_ref.shape[1], o_ref.shape[2]
        o_ref[0, :, 0:past] = c_ref[0, 0]
        o_ref[0, :, past:past + T] = n_ref[0, 0:rows, :]
        o_ref[0, :, past + T:width] = jnp.zeros((rows, width - past - T), f32)


def _join_caches(caches_t, layer, news_t, s_pad):
    B, _, T = news_t[0].shape
    past = caches_t[0].shape[3]
    return pl.pallas_call(
        functools.partial(_join_kernel, past=past, T=T),
        grid=(B,),
        in_specs=[pl.BlockSpec((1, 1) + c.shape[2:], lambda b: (layer, b, 0, 0)) for c in caches_t]
        + [pl.BlockSpec((1, LANES, T), lambda b: (b, 0, 0)) for _ in news_t],
        out_specs=[pl.BlockSpec((1, c.shape[2], s_pad), lambda b: (b, 0, 0)) for c in caches_t],
        out_shape=[jax.ShapeDtypeStruct((B, c.shape[2], s_pad), f32) for c in caches_t],
        compiler_params=pltpu.CompilerParams(dimension_semantics=("arbitrary",), vmem_limit_bytes=VMEM_LIMIT),
        name="join_caches",
    )(*caches_t, *news_t)


def _layer(x, conv_state, caches_t, wts):
    (layer, w_in, w_tail, conv_w, conv_b, cln_g, cln_b, w_o, ln1_g, ln1_b, w_gate_up, w_down, ln2_g, ln2_b) = wts
    B, T, _ = x.shape
    past = 0 if caches_t is None else caches_t[0].shape[3]
    N = B * T
    x2d = x.reshape(N, D_MODEL)
    u, q_hm, qi_hm, kt, vt, kwt = _inproj(x2d, layer, w_in, w_tail, T, past)

    u3 = u.reshape(B, T, C_CONV)
    conv_out = _conv(u3, conv_state, conv_w, conv_b, cln_g, cln_b)
    new_conv = jnp.concatenate([conv_state.astype(f32), u3], axis=1)[:, -(CONV_WIDTH - 1):]

    per_head = lambda a: jnp.transpose(a.reshape(B, N_KV_HEADS, HEAD_DIM, T), (0, 3, 1, 2))
    k, v = per_head(kt), per_head(vt)
    ki = jnp.transpose(kwt[:, :IDX_DIM, :], (0, 2, 1))
    wi2d = jnp.transpose(kwt[:, IDX_DIM:IDX_DIM + N_IDX_HEADS, :], (0, 2, 1)).reshape(N, N_IDX_HEADS)
    S = past + T
    s_pad = -(-S // 256) * 256
    if caches_t is None:
        assert s_pad == T
        kwt_all, kt_all, vt_all = kwt, kt, vt
    else:
        kwt_all, kt_all, vt_all = _join_caches(caches_t, layer, (kwt, kt, vt), s_pad)
    attn = _attention(qi_hm, wi2d, q_hm, kwt_all, kt_all, vt_all, B=B, T=T, past=past, topk=min(TOPK_MAX, S // 4))

    y = _post(x2d, conv_out.reshape(N, C_CONV), attn, layer, w_o, ln1_g, ln1_b, w_gate_up, w_down, ln2_g, ln2_b)
    return y.reshape(B, T, D_MODEL), k, v, ki, new_conv


def kernel(x_prompt, x_sample, cache_k, cache_v, cache_k_idx, state_conv, w_in, conv_w, conv_b, conv_ln_g,
           conv_ln_b, w_o, ln1_g, ln1_b, w_gate_up, w_down, ln2_g, ln2_b):
    Bp = x_prompt.shape[0]
    dt = x_prompt.dtype
    zero_conv = jnp.zeros((Bp, CONV_WIDTH - 1, C_CONV), dt)
    t_minor = lambda c: jnp.moveaxis(c.reshape(c.shape[:3] + (-1,)), 2, 3)
    caches_t = (t_minor(cache_k_idx), t_minor(cache_k), t_minor(cache_v))
    hp, hs = x_prompt, x_sample
    outs_p, outs_s = [], []
    for l in range(DEPTH):
        wts = (
            l, w_in, jnp.pad(w_in[l, :, _O_KW:].astype(bf16), ((0, 0), (0, _O_KW + LANES - IN_DIM))),
            conv_w[l], conv_b[l], conv_ln_g[l], conv_ln_b[l],
            w_o, ln1_g[l], ln1_b[l], w_gate_up, w_down, ln2_g[l], ln2_b[l],
        )
        hp, *op = _layer(hp, zero_conv, None, wts)
        hs, *os_ = _layer(hs, state_conv[l], caches_t, wts)
        outs_p.append(op)
        outs_s.append(os_)
    stack = lambda outs, i: jnp.stack([o[i] for o in outs])
    return (hp, hs,
            stack(outs_p, 0), stack(outs_p, 1), stack(outs_p, 2), stack(outs_p, 3),
            stack(outs_s, 0), stack(outs_s, 1), stack(outs_s, 2), stack(outs_s, 3))
```

```python
import functools
import math

import jax
import jax.numpy as jnp
from jax import lax
from jax.experimental import pallas as pl
from jax.experimental.pallas import tpu as pltpu

D_MODEL = 1024
CHUNK = 64
C_CONV = D_MODEL // 2
CONV_WIDTH = 31
HEAD_DIM = 64
N_HEADS = (D_MODEL // 2) // HEAD_DIM
N_KV_HEADS = 2
GROUP = N_HEADS // N_KV_HEADS
ATTN_WIDTH = N_HEADS * HEAD_DIM
N_IDX_HEADS = 8
IDX_DIM = 64
TOPK_MAX = 256
ROPE_THETA = 10000.0
D_FF = -(-8 * D_MODEL // (3 * 256)) * 256
IN_DIM = 2 * C_CONV + (N_HEADS + 2 * N_KV_HEADS) * HEAD_DIM + N_IDX_HEADS * IDX_DIM + IDX_DIM + N_IDX_HEADS
DEPTH = 2
ALPHA = (2 * DEPTH) ** 0.25
ATTN_SCALE = HEAD_DIM ** -0.5
IDX_SCALE = IDX_DIM ** -0.5
IDX_HEAD_SCALE = N_IDX_HEADS ** -0.5
LN_EPS = 1e-5
Q_SCALE = ATTN_SCALE * math.log2(math.e)

LANES = 128
SUBLANES = 8
VMEM_LIMIT = 56 * 1024 * 1024
IN_PAD = -(-IN_DIM // LANES) * LANES
HALO = 32
NEG_BIG = -1e30
INT_MIN = -(2 ** 31)
F32_LOWEST = float(jnp.finfo(jnp.float32).min)
NEG_INF_WORD = 0x007FFFFF
V_ROWS = HEAD_DIM + 16

_O_A = 0
_O_G = _O_A + C_CONV
_O_Q = _O_G + C_CONV
_O_K = _O_Q + ATTN_WIDTH
_O_V = _O_K + N_KV_HEADS * HEAD_DIM
_O_QI = _O_V + N_KV_HEADS * HEAD_DIM
_O_KW = _O_QI + N_IDX_HEADS * IDX_DIM

_NT = (((1,), (1,)), ((), ()))

f32 = jnp.float32
bf16 = jnp.bfloat16


def _layer_norm(y, g, b):
    mu = jnp.mean(y, axis=-1, keepdims=True)
    d = y - mu
    var = jnp.mean(d * d, axis=-1, keepdims=True)
    return d * lax.rsqrt(var + LN_EPS) * g + b


def _rope128(x, cos, sa, sb):
    return x * cos + pltpu.roll(x, 96, 1) * sa + pltpu.roll(x, 32, 1) * sb


def _inproj_kernel(x_ref, w_ref, wt_ref, cos_ref, sa_ref, sb_ref, cosk_ref, sak_ref, sbk_ref,
                   u_ref, q_ref, qi_ref, kt_ref, vt_ref, kwt_ref, *, seq):
    xb = x_ref[...].astype(bf16)
    tm = xb.shape[0]

    def put_t(ref, a):
        at = a.T
        for s in range(ref.shape[0]):
            ref[s] = at[:, s * seq:(s + 1) * seq] if tm > seq else at

    def mm(lo, width):
        return jnp.dot(xb, w_ref[0, :, lo:lo + width].astype(bf16), preferred_element_type=f32)

    u_ref[...] = mm(_O_A, C_CONV) * jax.nn.sigmoid(mm(_O_G, C_CONV))

    cos, sa, sb = cos_ref[...], sa_ref[...], sb_ref[...]
    low = lax.broadcasted_iota(jnp.int32, (tm, LANES), 1) < HEAD_DIM

    def place(r, src, dst):
        if src != dst:
            r = pltpu.roll(r, HEAD_DIM, 1)
        return jnp.where(low if dst == 0 else jnp.logical_not(low), r, 0.0).astype(bf16)

    q = mm(_O_Q, ATTN_WIDTH)
    qi = mm(_O_QI, N_IDX_HEADS * IDX_DIM)
    for c in range(ATTN_WIDTH // LANES):
        r = _rope128(q[:, c * LANES:(c + 1) * LANES], cos, sa, sb) * Q_SCALE
        for half in range(2):
            h = 2 * c + half
            q_ref[h] = place(r, half, h // GROUP)
        r = _rope128(qi[:, c * LANES:(c + 1) * LANES], cos, sa, sb)
        for half in range(2):
            qi_ref[2 * c + half] = place(r, half, 0)
    put_t(kt_ref, _rope128(mm(_O_K, N_KV_HEADS * HEAD_DIM), cos, sa, sb))
    put_t(vt_ref, mm(_O_V, N_KV_HEADS * HEAD_DIM))
    kw = jnp.dot(xb, wt_ref[...], preferred_element_type=f32)
    put_t(kwt_ref, _rope128(kw, cosk_ref[...], sak_ref[...], sbk_ref[...]))


def _rope_tables(T, past, rows):
    half = HEAD_DIM // 2
    inv = ROPE_THETA ** (-jnp.arange(half, dtype=f32) / half)
    pos = (jnp.arange(T, dtype=jnp.int32) + past).astype(f32)
    ang = pos[:, None] * inv[None, :]
    cos, sin = jnp.cos(ang), jnp.sin(ang)
    zero = jnp.zeros_like(sin)
    cos64 = jnp.concatenate([cos, cos], axis=1)
    sa64 = jnp.concatenate([-sin, zero], axis=1)
    sb64 = jnp.concatenate([zero, sin], axis=1)
    two = lambda a: jnp.concatenate([a, a], axis=1)
    wscale = jnp.full((T, N_IDX_HEADS), IDX_HEAD_SCALE * IDX_SCALE, f32)
    ones = jnp.ones((T, LANES - IDX_DIM - N_IDX_HEADS), f32)
    cosk = jnp.concatenate([cos64, wscale, ones], axis=1)
    zpad = jnp.zeros((T, LANES - IDX_DIM), f32)
    tabs = (two(cos64), two(sa64), two(sb64), cosk,
            jnp.concatenate([sa64, zpad], axis=1), jnp.concatenate([sb64, zpad], axis=1))
    reps = rows // T
    return tuple(jnp.tile(t, (reps, 1)) for t in tabs)


def _inproj(x2d, layer, w_in, w_tail, T, past):
    N = x2d.shape[0]
    tm = min(512, N)
    rows = max(T, tm)
    tabs = _rope_tables(T, past, rows)
    nt = rows // tm
    row_spec = lambda w: pl.BlockSpec((tm, w), lambda i: (i, 0))
    tab_spec = pl.BlockSpec((tm, LANES), lambda i: (i % nt, 0))
    hm_spec = pl.BlockSpec((N_HEADS, tm, LANES), lambda i: (0, i, 0))
    if tm <= T:
        per = T // tm
        t_spec = pl.BlockSpec((1, LANES, tm), lambda i: (i // per, 0, i % per))
    else:
        t_spec = pl.BlockSpec((tm // T, LANES, T), lambda i: (i, 0, 0))
    t_shape = jax.ShapeDtypeStruct((N // T, LANES, T), f32)
    return pl.pallas_call(
        functools.partial(_inproj_kernel, seq=T),
        grid=(N // tm,),
        in_specs=[row_spec(D_MODEL),
                  pl.BlockSpec((1, D_MODEL, IN_DIM), lambda i: (layer, 0, 0), pipeline_mode=pl.Buffered(1)),
                  pl.BlockSpec((D_MODEL, LANES), lambda i: (0, 0))] + [tab_spec] * 6,
        out_specs=[row_spec(C_CONV), hm_spec, hm_spec, t_spec, t_spec, t_spec],
        out_shape=[
            jax.ShapeDtypeStruct((N, C_CONV), f32),
            jax.ShapeDtypeStruct((N_HEADS, N, LANES), bf16),
            jax.ShapeDtypeStruct((N_IDX_HEADS, N, LANES), bf16),
            t_shape, t_shape, t_shape,
        ],
        compiler_params=pltpu.CompilerParams(dimension_semantics=("arbitrary",), vmem_limit_bytes=VMEM_LIMIT),
        name="inproj",
    )(x2d, w_in, w_tail, *tabs)


def _conv_kernel(u_ref, halo_ref, st_ref, w_ref, b_ref, g_ref, bt_ref, o_ref, f_ref, sh_ref, *, tc, rb):
    i = pl.program_id(1)
    f_ref[0:HALO, :] = jnp.where(i == 0, st_ref[0], halo_ref[0])
    f_ref[HALO:HALO + tc, :] = u_ref[0]
    span = tc + HALO - SUBLANES
    for c in range(1, SUBLANES):
        sh_ref[c, 0:span, :] = f_ref[c:c + span, :]
    lead = HALO - (CONV_WIDTH - 1)
    for r in range(tc // rb):
        acc = jnp.zeros((rb, C_CONV), f32)
        for j in range(CONV_WIDTH):
            c = (lead + j) % SUBLANES
            lo = r * rb + lead + j - c
            rows = f_ref[lo:lo + rb, :] if c == 0 else sh_ref[c, lo:lo + rb, :]
            acc = acc + rows * jnp.tile(w_ref[j], (rb // SUBLANES, 1))
        y = _layer_norm(acc + b_ref[...], g_ref[...], bt_ref[...])
        o_ref[0, r * rb:(r + 1) * rb, :] = (y * jax.nn.sigmoid(y)).astype(o_ref.dtype)


def _conv(u3, state, conv_w, conv_b, ln_g, ln_b):
    B, T, _ = u3.shape
    tc = min(256, T)
    st = jnp.pad(state.astype(f32), ((0, 0), (HALO - (CONV_WIDTH - 1), 0), (0, 0)))
    wp = jnp.broadcast_to(conv_w[:, None, :], (CONV_WIDTH, SUBLANES, C_CONV))
    vec = lambda a: a.reshape(1, C_CONV)
    hb = tc // HALO
    return pl.pallas_call(
        functools.partial(_conv_kernel, tc=tc, rb=32),
        grid=(B, T // tc),
        in_specs=[
            pl.BlockSpec((1, tc, C_CONV), lambda b, i: (b, i, 0)),
            pl.BlockSpec((1, HALO, C_CONV), lambda b, i: (b, jnp.maximum(i * hb - 1, 0), 0)),
            pl.BlockSpec((1, HALO, C_CONV), lambda b, i: (b, 0, 0)),
            pl.BlockSpec((CONV_WIDTH, SUBLANES, C_CONV), lambda b, i: (0, 0, 0)),
        ] + [pl.BlockSpec((1, C_CONV), lambda b, i: (0, 0))] * 3,
        out_specs=pl.BlockSpec((1, tc, C_CONV), lambda b, i: (b, i, 0)),
        out_shape=jax.ShapeDtypeStruct((B, T, C_CONV), bf16),
        scratch_shapes=[pltpu.VMEM((HALO + tc, C_CONV), f32),
                        pltpu.VMEM((SUBLANES, HALO + tc - SUBLANES, C_CONV), f32)],
        compiler_params=pltpu.CompilerParams(dimension_semantics=("arbitrary", "arbitrary"),
                                             vmem_limit_bytes=VMEM_LIMIT),
        name="conv",
    )(u3, u3, st, wp, vec(conv_b), vec(ln_g), vec(ln_b))


def _key_to_f32(key):
    bits = key ^ ((key >> 31) & jnp.int32(0x7FFFFFFF))
    return pltpu.bitcast(bits, f32)


def _attn_kernel(qi_ref, wi_ref, q_ref, kwt_ref, kt_ref, vt_ref, o_ref, sc_ref, m_ref, acc_ref, ot_ref, s_ref,
                 mx_ref, st_ref, planes_ref, cand_ref,
                 *, tq, hp, ts, t1, ta, past, topk):
    nvh = N_HEADS // hp
    lw = hp * tq
    lw2 = sc_ref.shape[1]
    j = pl.program_id(1)
    n_full = (past + j * tq) // ts
    n_chunks = n_full + 1
    kf = float(topk)

    def key_rows(ref, row0, n):
        x = ref[0, :, pl.ds(row0, n)].astype(f32)
        if x.shape[0] < LANES:
            x = jnp.concatenate([x, jnp.zeros((LANES - x.shape[0], n), f32)], axis=0)
        return x.T.astype(bf16)

    ones_rows = jnp.where(lax.broadcasted_iota(jnp.int32, (V_ROWS - HEAD_DIM, ta), 0) == 0, 1.0, 0.0).astype(bf16)

    q_chunk = (lax.broadcasted_iota(jnp.int32, (t1, lw2), 1) % tq) // CHUNK

    nc_max = cand_ref.shape[0]
    int_min = jnp.int32(INT_MIN)
    cand_ref[...] = jnp.zeros(cand_ref.shape, jnp.int32)

    def write_planes(c):
        row0 = pl.multiple_of(c * ts, ts)
        for lt in range(lw2 // LANES):
            lanes = slice(lt * LANES, (lt + 1) * LANES)
            bits = pltpu.bitcast(sc_ref[pl.ds(row0, ts), lanes], jnp.int32)
            u = bits ^ ((bits >> 31) | int_min)
            w = [u[i * SUBLANES:(i + 1) * SUBLANES, :] for i in range(32)]
            j, m = 16, 0x0000FFFF
            while j:
                k = 0
                while k < 32:
                    t = (w[k] ^ (w[k + j] >> j)) & jnp.int32(m)
                    w[k] = w[k] ^ t
                    w[k + j] = w[k + j] ^ (t << j)
                    k = (k + j + 1) & ~j
                j >>= 1
                m = m ^ (m << j)
            gone = jnp.full((SUBLANES, LANES), -1, jnp.int32)
            for b in range(32):
                plane = w[31 - b]
                planes_ref[b, c, :, lanes] = plane
                gone = gone & (plane if (NEG_INF_WORD >> b) & 1 else ~plane)
            cand_ref[c, :, lanes] = ~gone

    def score_chunk(c, masked):
        for r in range(ts // t1):
            row0 = pl.multiple_of(c * ts + r * t1, t1)
            kic = key_rows(kwt_ref, row0, t1)
            acc = jnp.zeros((t1, lw), f32)
            for hv in range(nvh):
                d = lax.dot_general(kic, qi_ref[hv], _NT, preferred_element_type=f32)
                acc = acc + jnp.maximum(d, 0.0) * wi_ref[0, hv:hv + 1, :]
            width = lw
            while width > tq:
                width //= 2
                acc = acc + pltpu.roll(acc, width, 1)
            acc = acc[:, :lw2]
            if masked:
                k_chunk = (lax.broadcasted_iota(jnp.int32, (t1, lw2), 0) + r * t1) // CHUNK
                acc = jnp.where(k_chunk <= q_chunk, acc, -jnp.inf)
            sc_ref[pl.ds(row0, t1), :] = acc
        write_planes(c)

    def pair_body(i, carry):
        score_chunk(2 * i, False)
        score_chunk(2 * i + 1, False)
        return carry

    lax.fori_loop(0, n_full // 2, pair_body, 0)

    @pl.when(n_full % 2 == 1)
    def _():
        score_chunk(n_full - 1, False)

    score_chunk(n_full, True)

    def count(preds):
        def body(c, accs):
            row0 = pl.multiple_of(c * ts, ts)
            blk = sc_ref[pl.ds(row0, ts), :]
            out = []
            for pred, acc in zip(preds, accs):
                m = jnp.where(pred(blk, row0), 1.0, 0.0)
                out.append(acc + m.reshape(ts // SUBLANES, SUBLANES, lw2).sum(axis=0))
            return tuple(out)
        init = tuple(jnp.zeros((SUBLANES, lw2), f32) for _ in preds)
        accs = lax.fori_loop(0, n_chunks, body, init)
        return [a.sum(axis=0, keepdims=True) for a in accs]

    def cut_counts(thr):
        c_ge, c_gt = count([lambda blk, row0: blk >= thr, lambda blk, row0: blk > thr])
        st_ref[0:1, :] = thr
        st_ref[1:2, :] = c_ge
        st_ref[2:3, :] = c_gt

    def idle_body(c, carry):
        planes_ref[:, c] = jnp.zeros((32, SUBLANES, lw2), jnp.int32)
        return carry

    lax.fori_loop(n_chunks, nc_max, idle_body, 0)

    def bit_count(b, flip):
        cnt = jnp.zeros((SUBLANES, lw2), jnp.int32)
        tot = jnp.zeros((SUBLANES, lw2), jnp.int32)
        for c in range(nc_max):
            cand = cand_ref[c]
            if flip is None:
                tot = tot + lax.population_count(cand)
            else:
                cand = cand & (planes_ref[b + 1, c] ^ flip)
                cand_ref[c] = cand
            cnt = cnt + lax.population_count(cand & planes_ref[b, c])
        red = lambda a: a.astype(f32).sum(axis=0, keepdims=True)
        return red(cnt), red(tot)

    def decide(c1, left, word, b):
        take = c1 >= left
        return (jnp.where(take, left, left - c1), word | jnp.where(take, lax.shift_left(jnp.int32(1), b), 0),
                jnp.where(take, 0, -1))

    c1, n_adm = bit_count(31, None)
    state = decide(c1, jnp.full((1, lw2), kf, f32), jnp.zeros((1, lw2), jnp.int32), 31)

    def radix_body(it, state):
        left, word, flip = state
        b = 30 - it
        c1, _ = bit_count(b, flip)
        return decide(c1, left, word, b)

    _, word, _ = lax.fori_loop(0, 31, radix_body, state)
    few = n_adm < kf
    cut_counts(jnp.where(few, F32_LOWEST, _key_to_f32(word ^ int_min)))

    good = jnp.logical_or(few, jnp.logical_and(st_ref[2:3, :] < kf, st_ref[1:2, :] >= kf))

    @pl.when(jnp.max(jnp.where(good, 0.0, 1.0)) > 0.0)
    def _():
        def bisect(it, key):
            cand = key + lax.shift_left(jnp.int32(1), 31 - it)
            thr = _key_to_f32(cand)
            cnt, = count([lambda blk, row0: blk >= thr])
            return jnp.where(cnt >= kf, cand, key)

        key = lax.fori_loop(0, 32, bisect, jnp.full((1, lw2), INT_MIN, jnp.int32))
        cut_counts(jnp.where(key == INT_MIN, F32_LOWEST, _key_to_f32(key)))

    thr, c_ge, c_gt = st_ref[0:1, :], st_ref[1:2, :], st_ref[2:3, :]
    ties = c_ge > kf
    need = kf - c_gt
    any_ties = jnp.max(jnp.where(ties, 1.0, 0.0))

    def write_bias(select):
        def body(c, carry):
            row0 = pl.multiple_of(c * ts, ts)
            blk = sc_ref[pl.ds(row0, ts), :]
            sc_ref[pl.ds(row0, ts), :] = jnp.where(select(blk, row0), 0.0, NEG_BIG)
            return carry
        lax.fori_loop(0, n_chunks, body, 0)

    @pl.when(any_ties == 0.0)
    def _():
        write_bias(lambda blk, row0: blk >= thr)

    @pl.when(any_ties > 0.0)
    def _():
        def rows(row0):
            return lax.broadcasted_iota(jnp.int32, (ts, lw2), 0) + row0

        tb = pltpu.bitcast(thr, jnp.int32)
        tw = tb ^ ((tb >> 31) | int_min)
        cand_ref[...] = jnp.full(cand_ref.shape, -1, jnp.int32)

        def tied_body(b, carry):
            flip = jnp.where(((tw >> b) & 1) == 1, 0, -1)
            for c in range(nc_max):
                cand_ref[c] = cand_ref[c] & (planes_ref[b, c] ^ flip)
            return carry

        lax.fori_loop(0, 32, tied_body, 0)

        sub = lax.broadcasted_iota(jnp.int32, (SUBLANES, lw2), 0)
        passes = [("chunk", k) for k in reversed(range(max(1, (nc_max - 1).bit_length())))]
        passes += [("group", k) for k in reversed(range(5))] + [("sublane", k) for k in reversed(range(3))]
        left, cut = need, jnp.zeros((1, lw2), jnp.int32)
        for kind, k in passes:
            def high(c, kind=kind, k=k):
                if kind == "chunk":
                    return jnp.int32(-1 if (c >> k) & 1 else 0)
                if kind == "group":
                    word = sum(1 << i for i in range(32) if ((31 - i) >> k) & 1)
                    return jnp.int32(word - (1 << 32) if word >= (1 << 31) else word)
                return jnp.where(((sub >> k) & 1) == 1, -1, 0)
            cnt = jnp.zeros((SUBLANES, lw2), jnp.int32)
            for c in range(nc_max):
                cnt = cnt + lax.population_count(cand_ref[c] & ~high(c))
            c0 = cnt.astype(f32).sum(axis=0, keepdims=True)
            low = c0 >= left
            left = jnp.where(low, left, left - c0)
            weight = {"chunk": 8, "group": 3, "sublane": 0}[kind] + k
            cut = cut | jnp.where(low, 0, 1 << weight)
            flip = jnp.where(low, -1, 0)
            for c in range(nc_max):
                cand_ref[c] = cand_ref[c] & (high(c) ^ flip)
        cut = jnp.where(ties, cut, jnp.int32(2 ** 30))
        write_bias(lambda blk, row0: jnp.logical_or(
            blk > thr, jnp.logical_and(blk == thr, rows(row0) <= cut)))

    m_ref[...] = jnp.full(m_ref.shape, NEG_BIG, f32)
    acc_ref[...] = jnp.zeros(acc_ref.shape, f32)

    n_steps = n_chunks * (ts // ta)
    last = n_steps - 1

    def logits(step, buf):
        row0 = pl.multiple_of(step * ta, ta)
        kc = key_rows(kt_ref, row0, ta)
        for hv in range(nvh):
            bias = sc_ref[pl.ds(row0, ta), :]
            if lw2 < lw:
                bias = jnp.concatenate([bias] * (lw // lw2), axis=1)
            s = lax.dot_general(kc, q_ref[hv], _NT, preferred_element_type=f32) + bias
            s_ref[buf, hv] = s
            mx_ref[buf, hv] = s.reshape(ta // SUBLANES, SUBLANES, lw).max(axis=0).max(axis=0, keepdims=True)

    def absorb(step, buf):
        row0 = pl.multiple_of(step * ta, ta)
        for hv in range(nvh):
            g = (hv * hp) // GROUP
            m_old = m_ref[hv]
            m_new = jnp.maximum(m_old, mx_ref[buf, hv])
            alpha = jnp.exp2(m_old - m_new)
            p = jnp.exp2(s_ref[buf, hv] - m_new).astype(bf16)
            vt = jnp.concatenate(
                [vt_ref[0, g * HEAD_DIM:(g + 1) * HEAD_DIM, pl.ds(row0, ta)].astype(bf16), ones_rows], axis=0)
            acc_ref[hv] = alpha * acc_ref[hv] + jnp.dot(vt, p, preferred_element_type=f32)
            m_ref[hv] = m_new

    logits(0, 0)

    def att_body(i, carry):
        logits(jnp.minimum(2 * i + 1, last), 1)
        absorb(2 * i, 0)
        logits(jnp.minimum(2 * i + 2, last), 0)
        absorb(2 * i + 1, 1)
        return carry

    lax.fori_loop(0, n_steps // 2, att_body, 0)

    @pl.when(n_steps % 2 == 1)
    def _():
        absorb(last, 0)

    for hv in range(nvh):
        o = acc_ref[hv]
        ot_ref[hv * HEAD_DIM:(hv + 1) * HEAD_DIM, :] = o[:HEAD_DIM] / o[HEAD_DIM:HEAD_DIM + 1]
    if hp == 1:
        o_ref[...] = ot_ref[...].T.astype(o_ref.dtype)
    else:
        o_ref[0] = ot_ref[...].astype(o_ref.dtype)


def _attn_kernel_cached(qi_ref, wi_ref, q_ref, ckw_hbm, ck_hbm, cv_hbm, nkw_ref, nk_ref, nv_ref, o_ref, *scratch,
                        layer, seq, **kw):
    *scratch, kw_buf, k_buf, v_buf, sem = scratch
    b = pl.program_id(0)
    past = kw["past"]
    copies = []
    for i, (src, dst) in enumerate(((ckw_hbm, kw_buf), (ck_hbm, k_buf), (cv_hbm, v_buf))):
        copies.append(pltpu.make_async_copy(src.at[layer, b], dst.at[0, :, pl.ds(0, past)], sem.at[i]))
        copies[-1].start()
    for cp in copies:
        cp.wait()
    for new, dst in ((nkw_ref, kw_buf), (nk_ref, k_buf), (nv_ref, v_buf)):
        rows, width = dst.shape[1], dst.shape[2]
        dst[0, :, past:past + seq] = new[0, 0:rows, :]
        dst[0, :, past + seq:width] = jnp.zeros((rows, width - past - seq), f32)
    _attn_kernel(qi_ref, wi_ref, q_ref, kw_buf, k_buf, v_buf, o_ref, *scratch, **kw)


def _attention(qi_hm, wi2d, q_hm, kwt_all, kt_all, vt_all, *, B, T, past, topk, caches=None):
    tq = min(256, T)
    ts = 256
    ta = 256
    hp = ts // tq
    assert past % ts == 0 and (tq == ts or tq == T) and tq % CHUNK == 0 and hp in (1, 2, 4)
    nvh = N_HEADS // hp
    lw = hp * tq
    lw2 = max(tq, LANES)
    nq = T // tq
    nb = B * nq
    s_pad = (past + T + ts - 1) // ts * ts
    N = B * T
    key_block = lambda rows, width: pl.BlockSpec((1, rows, width), lambda b, j: (b, 0, 0))
    if caches is None:
        assert kt_all.shape == vt_all.shape == (B, LANES, s_pad)
        assert kwt_all.shape in ((B, LANES, s_pad), (B, IDX_DIM, s_pad))
        body = _attn_kernel
        key_args = (kwt_all, kt_all, vt_all)
        key_specs = [key_block(kwt_all.shape[1], s_pad), key_block(LANES, s_pad), key_block(LANES, s_pad)]
        key_scratch = []
    else:
        caches_t, layer = caches
        assert nq == 1 and kwt_all.shape == kt_all.shape == vt_all.shape == (B, LANES, T)
        body = functools.partial(_attn_kernel_cached, layer=layer, seq=T)
        key_args = (*caches_t, kwt_all, kt_all, vt_all)
        key_specs = [pl.BlockSpec(memory_space=pl.ANY)] * 3 + [key_block(LANES, T)] * 3
        key_scratch = [pltpu.VMEM((1, c.shape[2], s_pad), f32) for c in caches_t] + [pltpu.SemaphoreType.DMA((3,))]

    def pack(a):
        if hp == 1:
            return a
        a = a.reshape(nvh, hp, nb, tq, LANES)
        return jnp.transpose(a, (0, 2, 1, 3, 4)).reshape(nvh, nb * lw, LANES)

    wi3 = jnp.transpose(wi2d.reshape(nb, tq, nvh, hp), (0, 2, 3, 1)).reshape(nb, nvh, lw)
    hm_spec = pl.BlockSpec((nvh, lw, LANES), lambda b, j: (0, b * nq + j, 0))
    if hp == 1:
        out_spec = pl.BlockSpec((tq, ATTN_WIDTH), lambda b, j: (b * nq + j, 0))
        out_shape = jax.ShapeDtypeStruct((N, ATTN_WIDTH), bf16)
    else:
        out_spec = pl.BlockSpec((1, nvh * HEAD_DIM, lw), lambda b, j: (b * nq + j, 0, 0))
        out_shape = jax.ShapeDtypeStruct((nb, nvh * HEAD_DIM, lw), bf16)
    out = pl.pallas_call(
        functools.partial(body, tq=tq, hp=hp, ts=ts, t1=128, ta=ta, past=past, topk=topk),
        grid=(B, nq),
        in_specs=[hm_spec, pl.BlockSpec((1, nvh, lw), lambda b, j: (b * nq + j, 0, 0)), hm_spec] + key_specs,
        out_specs=out_spec,
        out_shape=out_shape,
        scratch_shapes=[pltpu.VMEM((s_pad, lw2), f32), pltpu.VMEM((nvh, 1, lw), f32),
                        pltpu.VMEM((nvh, V_ROWS, lw), f32), pltpu.VMEM((nvh * HEAD_DIM, lw), f32),
                        pltpu.VMEM((2, nvh, ta, lw), f32), pltpu.VMEM((2, nvh, 1, lw), f32),
                        pltpu.VMEM((SUBLANES, lw2), f32),
                        pltpu.VMEM((32, s_pad // ts, SUBLANES, lw2), jnp.int32),
                        pltpu.VMEM((s_pad // ts, SUBLANES, lw2), jnp.int32)] + key_scratch,
        compiler_params=pltpu.CompilerParams(dimension_semantics=("arbitrary", "arbitrary"),
                                             vmem_limit_bytes=VMEM_LIMIT),
        name="dsa_attention",
    )(pack(qi_hm), wi3, pack(q_hm), *key_args)
    if hp > 1:
        out = out.reshape(nb, nvh, HEAD_DIM, hp, tq)
        out = jnp.transpose(out, (0, 4, 1, 3, 2)).reshape(N, ATTN_WIDTH)
    return out


FF_CHUNK = 256


def _post_kernel(x_ref, c_ref, a_ref, wo_ref, g1_ref, b1_ref, wgu_ref, wd_ref, g2_ref, b2_ref, o_ref, acc_ref):
    w16 = lambda r, rows, cols: r[0, rows, cols].astype(bf16)
    full = slice(None)
    mix = (jnp.dot(c_ref[...], w16(wo_ref, slice(0, C_CONV), full), preferred_element_type=f32)
           + jnp.dot(a_ref[...], w16(wo_ref, slice(C_CONV, D_MODEL), full), preferred_element_type=f32))
    x1 = _layer_norm(ALPHA * x_ref[...] + mix, g1_ref[...], b1_ref[...])
    x1b = x1.astype(bf16)
    for c in range(D_FF // FF_CHUNK):
        lo = c * FF_CHUNK
        gate = jnp.dot(x1b, w16(wgu_ref, full, slice(lo, lo + FF_CHUNK)), preferred_element_type=f32)
        up = jnp.dot(x1b, w16(wgu_ref, full, slice(D_FF + lo, D_FF + lo + FF_CHUNK)), preferred_element_type=f32)
        act = (gate * jax.nn.sigmoid(gate) * up).astype(bf16)
        part = jnp.dot(act, w16(wd_ref, slice(lo, lo + FF_CHUNK), full), preferred_element_type=f32)
        if c == 0:
            acc_ref[...] = part
        else:
            acc_ref[...] += part
    o_ref[...] = _layer_norm(ALPHA * x1 + acc_ref[...], g2_ref[...], b2_ref[...])


def _post(x2d, conv2d, attn2d, layer, w_o, g1, b1, w_gate_up, w_down, g2, b2):
    N = x2d.shape[0]
    tm = min(512, N)
    row = lambda w: pl.BlockSpec((tm, w), lambda i: (i, 0))
    vec_spec = pl.BlockSpec((1, D_MODEL), lambda i: (0, 0))
    weight = lambda a: pl.BlockSpec((1,) + a.shape[1:], lambda i: (layer, 0, 0), pipeline_mode=pl.Buffered(1))
    vec = lambda a: a.reshape(1, D_MODEL)
    return pl.pallas_call(
        _post_kernel,
        grid=(N // tm,),
        in_specs=[row(D_MODEL), row(C_CONV), row(ATTN_WIDTH), weight(w_o), vec_spec, vec_spec,
                  weight(w_gate_up), weight(w_down), vec_spec, vec_spec],
        out_specs=row(D_MODEL),
        out_shape=jax.ShapeDtypeStruct((N, D_MODEL), f32),
        scratch_shapes=[pltpu.VMEM((tm, D_MODEL), f32)],
        compiler_params=pltpu.CompilerParams(dimension_semantics=("arbitrary",), vmem_limit_bytes=VMEM_LIMIT),
        name="post",
    )(x2d, conv2d, attn2d, w_o, vec(g1), vec(b1), w_gate_up, w_down, vec(g2), vec(b2))


def _join_kernel(*refs, past, T):
    n = len(refs) // 3
    for c_ref, n_ref, o_ref in zip(refs[:n], refs[n:2 * n], refs[2 * n:]):
        rows, width = o_ref.shape[1], o_ref.shape[2]
        o_ref[0, :, 0:past] = c_ref[0, 0]
        o_ref[0, :, past:past + T] = n_ref[0, 0:rows, :]
        o_ref[0, :, past + T:width] = jnp.zeros((rows, width - past - T), f32)


def _join_caches(caches_t, layer, news_t, s_pad):
    B, _, T = news_t[0].shape
    past = caches_t[0].shape[3]
    return pl.pallas_call(
        functools.partial(_join_kernel, past=past, T=T),
        grid=(B,),
        in_specs=[pl.BlockSpec((1, 1) + c.shape[2:], lambda b: (layer, b, 0, 0)) for c in caches_t]
        + [pl.BlockSpec((1, LANES, T), lambda b: (b, 0, 0)) for _ in news_t],
        out_specs=[pl.BlockSpec((1, c.shape[2], s_pad), lambda b: (b, 0, 0)) for c in caches_t],
        out_shape=[jax.ShapeDtypeStruct((B, c.shape[2], s_pad), f32) for c in caches_t],
        compiler_params=pltpu.CompilerParams(dimension_semantics=("arbitrary",), vmem_limit_bytes=VMEM_LIMIT),
        name="join_caches",
    )(*caches_t, *news_t)


def _layer(x, conv_state, caches_t, wts):
    (layer, w_in, w_tail, conv_w, conv_b, cln_g, cln_b, w_o, ln1_g, ln1_b, w_gate_up, w_down, ln2_g, ln2_b) = wts
    B, T, _ = x.shape
    past = 0 if caches_t is None else caches_t[0].shape[3]
    N = B * T
    x2d = x.reshape(N, D_MODEL)
    u, q_hm, qi_hm, kt, vt, kwt = _inproj(x2d, layer, w_in, w_tail, T, past)

    u3 = u.reshape(B, T, C_CONV)
    conv_out = _conv(u3, conv_state, conv_w, conv_b, cln_g, cln_b)
    new_conv = jnp.concatenate([conv_state.astype(f32), u3], axis=1)[:, -(CONV_WIDTH - 1):]

    per_head = lambda a: jnp.transpose(a.reshape(B, N_KV_HEADS, HEAD_DIM, T), (0, 3, 1, 2))
    k, v = per_head(kt), per_head(vt)
    ki = jnp.transpose(kwt[:, :IDX_DIM, :], (0, 2, 1))
    wi2d = jnp.transpose(kwt[:, IDX_DIM:IDX_DIM + N_IDX_HEADS, :], (0, 2, 1)).reshape(N, N_IDX_HEADS)
    S = past + T
    s_pad = -(-S // 256) * 256
    assert caches_t is not None or s_pad == T
    attn = _attention(qi_hm, wi2d, q_hm, kwt, kt, vt, B=B, T=T, past=past, topk=min(TOPK_MAX, S // 4),
                      caches=None if caches_t is None else (caches_t, layer))

    y = _post(x2d, conv_out.reshape(N, C_CONV), attn, layer, w_o, ln1_g, ln1_b, w_gate_up, w_down, ln2_g, ln2_b)
    return y.reshape(B, T, D_MODEL), k, v, ki, new_conv


def kernel(x_prompt, x_sample, cache_k, cache_v, cache_k_idx, state_conv, w_in, conv_w, conv_b, conv_ln_g,
           conv_ln_b, w_o, ln1_g, ln1_b, w_gate_up, w_down, ln2_g, ln2_b):
    Bp = x_prompt.shape[0]
    dt = x_prompt.dtype
    zero_conv = jnp.zeros((Bp, CONV_WIDTH - 1, C_CONV), dt)
    t_minor = lambda c: jnp.moveaxis(c.reshape(c.shape[:3] + (-1,)), 2, 3)
    caches_t = (t_minor(cache_k_idx), t_minor(cache_k), t_minor(cache_v))
    hp, hs = x_prompt, x_sample
    outs_p, outs_s = [], []
    for l in range(DEPTH):
        wts = (
            l, w_in, jnp.pad(w_in[l, :, _O_KW:].astype(bf16), ((0, 0), (0, _O_KW + LANES - IN_DIM))),
            conv_w[l], conv_b[l], conv_ln_g[l], conv_ln_b[l],
            w_o, ln1_g[l], ln1_b[l], w_gate_up, w_down, ln2_g[l], ln2_b[l],
        )
        hp, *op = _layer(hp, zero_conv, None, wts)
        hs, *os_ = _layer(hs, state_conv[l], caches_t, wts)
        outs_p.append(op)
        outs_s.append(os_)
    stack = lambda outs, i: jnp.stack([o[i] for o in outs])
    return (hp, hs,
            stack(outs_p, 0), stack(outs_p, 1), stack(outs_p, 2), stack(outs_p, 3),
            stack(outs_s, 0), stack(outs_s, 1), stack(outs_s, 2), stack(outs_s, 3))
```
